```python
import jax
import jax.numpy as jnp
from jax import lax
import numpy as np

D_MODEL = 1024
BATCH = 2
SEQ = 8192
DEPTH = 4

N_HEADS_A = 8
HEAD_DIM_A = 64
ROT_DIM = HEAD_DIM_A // 4
ROPE_THETA = 500000.0
DILATED_BRANCHES = ((128, 1), (512, 4), (2048, 16))
N_HEADS_B = 4
DV_B = 128
DK_B = DV_B // 2
GATE_RANK = 16
GATE_TAU = 16.0
GLA_CHUNK = 64
D_SGU = D_MODEL
N_GROUPS_C = 8
GROUP_C = D_SGU // N_GROUPS_C
SGU_CHUNK = 128
D_FF = 2816
CONV_W = 3

A_WIDTH = N_HEADS_A * HEAD_DIM_A
B_QK = N_HEADS_B * DK_B
B_V = N_HEADS_B * DV_B
AB_SPLITS = (A_WIDTH, A_WIDTH, A_WIDTH, B_QK, B_QK, B_V, B_V, GATE_RANK)
W_IN_AB = 3 * A_WIDTH + 2 * B_QK + 2 * B_V + GATE_RANK
NEG = -1e30
EPS = 1e-6

kernel_name = 'hybrid_dilated_gla_sgu_convffn'


def rms_norm(x, g):
    xf = x.astype(jnp.float32)
    y = xf * lax.rsqrt(jnp.mean(xf * xf, axis=-1, keepdims=True) + EPS)
    return (y * g.astype(jnp.float32)).astype(x.dtype)


def split_heads(t, n):
    b, s, _ = t.shape
    return t.reshape(b, s, n, -1).transpose(0, 2, 1, 3)


def merge_heads(t):
    b, h, s, d = t.shape
    return t.transpose(0, 2, 1, 3).reshape(b, s, h * d)


def partial_rope(x, positions):
    half = ROT_DIM // 2
    inv = ROPE_THETA ** (-jnp.arange(half, dtype=jnp.float32) * (2.0 / ROT_DIM))
    ang = positions.astype(jnp.float32)[:, None, :, None] * inv
    cos, sin = jnp.cos(ang), jnp.sin(ang)
    xr = x[..., :ROT_DIM].astype(jnp.float32)
    x1, x2 = xr[..., :half], xr[..., half:]
    rot = jnp.concatenate([x1 * cos - x2 * sin, x2 * cos + x1 * sin], axis=-1)
    return jnp.concatenate([rot.astype(x.dtype), x[..., ROT_DIM:]], axis=-1)


def dilated_branch(q, k, v, window, dilation):
    B, H, S, Dh = q.shape
    blk = window // dilation
    span = blk * dilation
    Sp = -(-S // span) * span
    nb = Sp // span

    def to_blocks(t):
        t = jnp.pad(t, ((0, 0), (0, 0), (0, Sp - S), (0, 0)))
        t = t.reshape(B, H, nb * blk, dilation, Dh).transpose(0, 1, 3, 2, 4)
        return t.reshape(B, H, dilation, nb, blk, Dh)

    def with_prev(t):
        prev = jnp.pad(t, ((0, 0), (0, 0), (0, 0), (1, 0), (0, 0), (0, 0)))[:, :, :, :-1]
        return jnp.concatenate([prev, t], axis=4)

    qb = to_blocks(q)
    kk = with_prev(to_blocks(k))
    vv = with_prev(to_blocks(v))
    s = jnp.einsum('bhrnqd,bhrnkd->bhrnqk', qb, kk,
                   preferred_element_type=jnp.float32) * (Dh ** -0.5)
    a = jnp.arange(blk)[:, None]
    c = jnp.arange(2 * blk)[None, :]
    band = (c >= a) & (c <= a + blk)
    valid_prev = (jnp.arange(nb)[:, None, None] > 0) | (c[None] >= blk)
    mask = band[None] & valid_prev
    s = jnp.where(mask, s, NEG)
    m = jnp.max(s, axis=-1)
    p = jnp.exp(s - m[..., None])
    l = jnp.sum(p, axis=-1)
    o = jnp.einsum('bhrnqk,bhrnkd->bhrnqd', p.astype(vv.dtype), vv,
                   preferred_element_type=jnp.float32) / l[..., None]

    def from_blocks(t):
        x_ = t.shape[-1]
        t = t.reshape(B, H, dilation, nb * blk, x_).transpose(0, 1, 3, 2, 4)
        return t.reshape(B, H, Sp, x_)[:, :, :S]

    return from_blocks(o), from_blocks(m[..., None])[..., 0], from_blocks(l[..., None])[..., 0]


def dilated_attention(q, k, v):
    outs = [dilated_branch(q, k, v, w, d) for (w, d) in DILATED_BRANCHES]
    ms = jnp.stack([m for _, m, _ in outs])
    ls = jnp.stack([l for _, _, l in outs])
    os_ = jnp.stack([o for o, _, _ in outs])
    wts = ls * jnp.exp(ms - jnp.max(ms, axis=0, keepdims=True))
    o = jnp.einsum('gbhs,gbhsd->bhsd', wts, os_) / jnp.sum(wts, axis=0)[..., None]
    return o


def gla(q, k, v, log_a):
    B, H, S, DK = q.shape
    DV = v.shape[-1]
    C = GLA_CHUNK
    N = S // C
    f32 = jnp.float32
    q = q.astype(f32).reshape(B, H, N, C, DK) * (DK ** -0.5)
    k = k.astype(f32).reshape(B, H, N, C, DK)
    v = v.astype(f32).reshape(B, H, N, C, DV)
    b = jnp.cumsum(log_a.astype(f32).reshape(B, H, N, C, DK), axis=3)
    b_last = b[:, :, :, C - 1:C]
    b_ref = b[:, :, :, C // 2 - 1:C // 2]
    att = jnp.einsum('bhnik,bhnjk->bhnij', q * jnp.exp(b - b_ref), k * jnp.exp(b_ref - b))
    att = jnp.where(jnp.tril(jnp.ones((C, C), dtype=bool)), att, 0.0)
    o = jnp.einsum('bhnij,bhnjv->bhniv', att, v)
    chunk_kv = jnp.einsum('bhnjk,bhnjv->bhnkv', k * jnp.exp(b_last - b), v)
    decay = jnp.exp(b_last[:, :, :, 0])

    def step(state, inp):
        kv, dec = inp
        return state * dec[..., None] + kv, state

    _, states = lax.scan(step, jnp.zeros((B, H, DK, DV), f32),
                         (chunk_kv.transpose(2, 0, 1, 3, 4), decay.transpose(2, 0, 1, 3)))
    states = states.transpose(1, 2, 0, 3, 4)
    o = o + jnp.einsum('bhnik,bhnkv->bhniv', q * jnp.exp(b), states)
    return o.reshape(B, H, S, DV)


def mixer_ab(h, positions, w_in, w_gate_up, b_gate_up, g_out_b, w_out):
    z = h @ w_in
    offs = []
    acc = 0
    for size in AB_SPLITS[:-1]:
        acc += size
        offs.append(acc)
    qa, ka, va, qb, kb, vb, rb, gl = jnp.split(z, offs, axis=-1)
    qa = partial_rope(split_heads(qa, N_HEADS_A), positions)
    ka = partial_rope(split_heads(ka, N_HEADS_A), positions)
    oa = dilated_attention(qa, ka, split_heads(va, N_HEADS_A))
    g = (gl @ w_gate_up + b_gate_up).astype(jnp.float32)
    log_a = jax.nn.log_sigmoid(g) / GATE_TAU
    ob = gla(split_heads(qb, N_HEADS_B), split_heads(kb, N_HEADS_B),
             split_heads(vb, N_HEADS_B), split_heads(log_a, N_HEADS_B))
    ob = rms_norm(ob, g_out_b) * jax.nn.silu(split_heads(rb, N_HEADS_B).astype(jnp.float32))
    o = jnp.concatenate([merge_heads(oa), merge_heads(ob)], axis=-1).astype(h.dtype)
    return o @ w_out


def mixer_c(h, w_in_c, g_sgu, w_spatial, b_spatial, w_out_c):
    B, S, _ = h.shape
    z = jax.nn.gelu(h @ w_in_c)
    u, v = jnp.split(z, 2, axis=-1)
    v = rms_norm(v, g_sgu).reshape(B, S // SGU_CHUNK, SGU_CHUNK, N_GROUPS_C, GROUP_C)
    ws = w_spatial * jnp.tril(jnp.ones((SGU_CHUNK, SGU_CHUNK), dtype=w_spatial.dtype))
    sv = jnp.einsum('gts,bnsgc->bntgc', ws, v) + b_spatial.T[:, :, None]
    return (u * sv.reshape(B, S, D_SGU)) @ w_out_c


def conv_ffn(h, w_up, conv_w, conv_b, w_down):
    S = h.shape[1]
    z = h @ w_up
    zp = jnp.pad(z, ((0, 0), (CONV_W - 1, 0), (0, 0)))
    z = sum(zp[:, j:j + S] * conv_w[j] for j in range(CONV_W)) + conv_b
    gate, up = jnp.split(z, 2, axis=-1)
    return (jax.nn.silu(gate) * up) @ w_down


def setup_inputs(seed: int = 0) -> dict:
    key = jax.random.key(seed)
    ks = jax.random.split(key, 24)
    f32 = jnp.float32
    n_even = (DEPTH + 1) // 2
    n_odd = DEPTH // 2
    out_scale = (2 * DEPTH) ** -0.5

    def normal(k, shape, scale):
        return jax.random.normal(k, shape, f32) * scale

    def gain(k, shape):
        return 1.0 + 0.02 * jax.random.normal(k, shape, f32)

    x = normal(ks[0], (BATCH, SEQ, D_MODEL), 1.0)
    positions = (jnp.arange(SEQ, dtype=jnp.int32)[None, :]
                 + jax.random.randint(ks[1], (BATCH, 1), 0, 4096, dtype=jnp.int32))
    return {
        'x': x,
        'positions': positions,
        'norm_mix': gain(ks[2], (DEPTH, D_MODEL)),
        'norm_ffn': gain(ks[3], (DEPTH, D_MODEL)),
        'w_in_ab': normal(ks[4], (n_even, D_MODEL, W_IN_AB), D_MODEL ** -0.5),
        'w_gate_up': normal(ks[5], (n_even, GATE_RANK, B_QK), GATE_RANK ** -0.5),
        'b_gate_up': normal(ks[6], (n_even, B_QK), 0.1),
        'g_out_b': gain(ks[7], (n_even, DV_B)),
        'w_out_ab': normal(ks[8], (n_even, D_MODEL, D_MODEL), D_MODEL ** -0.5 * out_scale),
        'w_in_c': normal(ks[9], (n_odd, D_MODEL, 2 * D_SGU), D_MODEL ** -0.5),
        'g_sgu': gain(ks[10], (n_odd, D_SGU)),
        'w_spatial': normal(ks[11], (n_odd, N_GROUPS_C, SGU_CHUNK, SGU_CHUNK), SGU_CHUNK ** -0.5),
        'b_spatial': 1.0 + normal(ks[12], (n_odd, N_GROUPS_C, SGU_CHUNK), 0.05),
        'w_out_c': normal(ks[13], (n_odd, D_SGU, D_MODEL), D_SGU ** -0.5 * out_scale),
        'w_up': normal(ks[14], (DEPTH, D_MODEL, 2 * D_FF), D_MODEL ** -0.5),
        'conv_w': normal(ks[15], (DEPTH, CONV_W, 2 * D_FF), CONV_W ** -0.5),
        'conv_b': normal(ks[16], (DEPTH, 2 * D_FF), 0.01),
        'w_down': normal(ks[17], (DEPTH, D_FF, D_MODEL), D_FF ** -0.5 * out_scale),
        'norm_final': gain(ks[18], (D_MODEL,)),
    }


def reference(x, positions, norm_mix, norm_ffn, w_in_ab, w_gate_up, b_gate_up, g_out_b,
              w_out_ab, w_in_c, g_sgu, w_spatial, b_spatial, w_out_c, w_up, conv_w, conv_b,
              w_down, norm_final):
    h = x
    for layer in range(DEPTH):
        i = layer // 2
        hn = rms_norm(h, norm_mix[layer])
        if layer % 2 == 0:
            h = h + mixer_ab(hn, positions, w_in_ab[i], w_gate_up[i], b_gate_up[i],
                             g_out_b[i], w_out_ab[i])
        else:
            h = h + mixer_c(hn, w_in_c[i], g_sgu[i], w_spatial[i], b_spatial[i], w_out_c[i])
        h = h + conv_ffn(rms_norm(h, norm_ffn[layer]), w_up[layer], conv_w[layer],
                         conv_b[layer], w_down[layer])
    return rms_norm(h, norm_final)
```

```python
import functools

import numpy as np
import jax
import jax.numpy as jnp
from jax import lax
from jax.experimental import pallas as pl
from jax.experimental.pallas import tpu as pltpu

F32 = jnp.float32
BF16 = jnp.bfloat16

D_MODEL = 1024
N_HEADS_A = 8
HEAD_DIM_A = 64
ROT_DIM = HEAD_DIM_A // 4
ROPE_THETA = 500000.0
DILATIONS = (1, 4, 16)
ATT_BLK = 128
ATT_SPAN = ATT_BLK * DILATIONS[-1]
N_HEADS_B = 4
DV_B = 128
DK_B = 64
GATE_RANK = 16
GATE_TAU = 16.0
GLA_CHUNK = 64
N_GROUPS_C = 8
SGU_CHUNK = 128
D_FF = 2816
CONV_W = 3
A_WIDTH = N_HEADS_A * HEAD_DIM_A
B_QK = N_HEADS_B * DK_B
B_V = N_HEADS_B * DV_B
NEG = -1e30
EPS = 1e-6
LANES = 128
FF_CHUNK = 256
N_FF_CHUNKS = D_FF // FF_CHUNK

VMEM_LIMIT = 56 * 1024 * 1024


def _dot(a, b):
    return jnp.dot(a, b, preferred_element_type=F32)


def _dot_nt(a, b):
    return lax.dot_general(a, b, (((1,), (1,)), ((), ())), preferred_element_type=F32)


def _rms(x):
    return x * lax.rsqrt(jnp.mean(x * x, axis=-1, keepdims=True) + EPS)


def _const_spec(shape):
    nd = len(shape)
    return pl.BlockSpec(shape, lambda *_: (0,) * nd, pipeline_mode=pl.Buffered(1))


def _params(sem, vmem=VMEM_LIMIT):
    return pltpu.CompilerParams(dimension_semantics=sem, vmem_limit_bytes=vmem)


def _rope_tab_kernel(pos_ref, inv_ref, sgn_ref, c_ref, s_ref):
    ang = pos_ref[...].astype(F32) * inv_ref[...]
    c_ref[...] = jnp.cos(ang)
    s_ref[...] = sgn_ref[...] * jnp.sin(ang)


def _rope_tables(positions):
    t = positions.size
    half = ROT_DIM // 2
    inv = np.float64(ROPE_THETA) ** (-np.arange(half, dtype=np.float64) * (2.0 / ROT_DIM))
    dim = np.arange(LANES) % HEAD_DIM_A
    inv_lane = np.where(dim < ROT_DIM, inv[dim % half], 0.0).astype(np.float32)[None, :]
    sgn_lane = np.where(dim < half, -1.0, np.where(dim < ROT_DIM, 1.0, 0.0)).astype(np.float32)[None, :]
    tm = 1024
    pos = positions.reshape(t, 1)
    return pl.pallas_call(
        _rope_tab_kernel,
        grid=(t // tm,),
        in_specs=[pl.BlockSpec((tm, 1), lambda i: (i, 0)),
                  _const_spec((1, LANES)), _const_spec((1, LANES))],
        out_specs=[pl.BlockSpec((tm, LANES), lambda i: (i, 0))] * 2,
        out_shape=[jax.ShapeDtypeStruct((t, LANES), F32)] * 2,
        compiler_params=_params(("arbitrary",)),
        name="rope_tables",
    )(pos, jnp.asarray(inv_lane), jnp.asarray(sgn_lane))


def _proj_ab_kernel(h_ref, g_ref, wa_ref, wb_ref, wg_ref, wgu_ref, bgu_ref, c_ref, s_ref,
                    qa_ref, ka_ref, va_ref, qb_ref, kb_ref, vb_ref, rb_ref, la_ref, hn_ref):
    hn_ref[...] = (_rms(h_ref[...]) * g_ref[...]).astype(BF16)
    cos = c_ref[...]
    sin = s_ref[...]
    lane = lax.broadcasted_iota(jnp.int32, cos.shape, 1)
    first_half = (lane % HEAD_DIM_A) < (ROT_DIM // 2)

    def rope(x):
        partner = jnp.where(first_half,
                            pltpu.roll(x, LANES - ROT_DIM // 2, axis=1),
                            pltpu.roll(x, ROT_DIM // 2, axis=1))
        return x * cos + partner * sin

    for j in range(A_WIDTH // LANES):
        sl = slice(j * LANES, (j + 1) * LANES)
        q = _dot(hn_ref[...], wa_ref[:, j * LANES:(j + 1) * LANES])
        qa_ref[:, sl] = rope(q).astype(BF16)
        k = _dot(hn_ref[...], wa_ref[:, A_WIDTH + j * LANES:A_WIDTH + (j + 1) * LANES])
        ka_ref[:, sl] = rope(k).astype(BF16)
    va_ref[...] = _dot(hn_ref[...], wa_ref[:, 2 * A_WIDTH:]).astype(BF16)
    qb_ref[...] = _dot(hn_ref[...], wb_ref[:, :B_QK])
    kb_ref[...] = _dot(hn_ref[...], wb_ref[:, B_QK:2 * B_QK])
    vb_ref[...] = _dot(hn_ref[...], wb_ref[:, 2 * B_QK:2 * B_QK + B_V]).astype(BF16)
    rb_ref[...] = _dot(hn_ref[...], wb_ref[:, 2 * B_QK + B_V:])
    gl = _dot(hn_ref[...], wg_ref[...])
    g = _dot(gl.astype(BF16), wgu_ref[...]) + bgu_ref[...]
    la_ref[...] = (jnp.minimum(g, 0.0) - jnp.log1p(jnp.exp(-jnp.abs(g)))) * (1.0 / GATE_TAU)


def _proj_ab(h, g, wa, wb, wg, wgu, bgu, cos, sin, tm=512):
    t = h.shape[0]
    row = lambda n: pl.BlockSpec((tm, n), lambda i: (i, 0))
    out_widths = (A_WIDTH, A_WIDTH, A_WIDTH, B_QK, B_QK, B_V, B_V, B_QK)
    out_dtypes = (BF16, BF16, BF16, F32, F32, BF16, F32, F32)
    return pl.pallas_call(
        _proj_ab_kernel,
        grid=(t // tm,),
        in_specs=[row(D_MODEL), _const_spec((1, D_MODEL)), _const_spec(wa.shape), _const_spec(wb.shape),
                  _const_spec(wg.shape), _const_spec(wgu.shape), _const_spec(bgu.shape),
                  row(LANES), row(LANES)],
        out_specs=[row(n) for n in out_widths],
        out_shape=[jax.ShapeDtypeStruct((t, n), d) for n, d in zip(out_widths, out_dtypes)],
        scratch_shapes=[pltpu.VMEM((tm, D_MODEL), BF16)],
        compiler_params=_params(("arbitrary",)),
        name="proj_ab",
    )(h, g, wa, wb, wg, wgu, bgu, cos, sin)


def _attn_unit(in_refs, first, acc_ref, lse_ref, rows):
    q_ref, kp_ref, kc_ref, vp_ref, vc_ref = in_refs
    row = lax.broadcasted_iota(jnp.int32, (ATT_BLK, ATT_BLK), 0)
    col = lax.broadcasted_iota(jnp.int32, (ATT_BLK, ATT_BLK), 1)
    band_prev = col >= row
    band_cur = col <= row
    lane = lax.broadcasted_iota(jnp.int32, (ATT_BLK, LANES), 1)
    low_head = lane < HEAD_DIM_A
    zero = jnp.zeros((), BF16)
    for hp in range(A_WIDTH // LANES):
        sl = slice(hp * LANES, (hp + 1) * LANES)
        q2, kp2, kc2, vp2, vc2 = (ref[0, :, sl] for ref in (q_ref, kp_ref, kc_ref, vp_ref, vc_ref))
        o_pair, lse_pair = [], []
        for low in (True, False):
            qm = jnp.where(low_head == low, q2, zero)
            sp = _dot_nt(qm, kp2) * (HEAD_DIM_A ** -0.5)
            sc = _dot_nt(qm, kc2) * (HEAD_DIM_A ** -0.5)
            sp = jnp.where(band_prev, sp, NEG)
            sp = jnp.where(first, NEG, sp)
            sc = jnp.where(band_cur, sc, NEG)
            m = jnp.maximum(jnp.max(sp, axis=-1, keepdims=True), jnp.max(sc, axis=-1, keepdims=True))
            pp = jnp.exp(sp - m)
            pc = jnp.exp(sc - m)
            l = jnp.sum(pp, axis=-1, keepdims=True) + jnp.sum(pc, axis=-1, keepdims=True)
            pv = _dot(pp.astype(BF16), vp2) + _dot(pc.astype(BF16), vc2)
            o_pair.append(pv / l)
            lse_pair.append(jnp.broadcast_to(m + jnp.log(l), (ATT_BLK, LANES)))
        acc_ref[hp, rows, :] = jnp.where(low_head, o_pair[0], o_pair[1])
        lse_ref[hp, rows, :] = jnp.where(low_head, lse_pair[0], lse_pair[1])


def _attn_kernel(*refs):
    nb = len(DILATIONS)
    in_refs = refs[:5 * nb]
    out_ref = refs[5 * nb]
    acc_refs = refs[5 * nb + 1:5 * nb + 1 + nb]
    lse_refs = refs[5 * nb + 1 + nb:]
    n = pl.program_id(1)
    r = pl.program_id(2)
    n_res = DILATIONS[-1]
    for g, d in enumerate(DILATIONS):
        blk = n * (n_res // d) + r // d
        if d == 1:
            rows = pl.ds(pl.multiple_of(r * ATT_BLK, ATT_BLK), ATT_BLK)
        else:
            rows = pl.ds((r // d) * (ATT_BLK * d) + r % d, ATT_BLK, stride=d)
        _attn_unit(in_refs[5 * g:5 * g + 5], blk == 0, acc_refs[g], lse_refs[g], rows)

    @pl.when(r == n_res - 1)
    def _():
        chunk = 256
        for hp in range(A_WIDTH // LANES):
            def body(i, carry):
                rows = pl.ds(pl.multiple_of(i * chunk, chunk), chunk)
                lse = [ref[hp, rows, :] for ref in lse_refs]
                top = functools.reduce(jnp.maximum, lse)
                w = [jnp.exp(x - top) for x in lse]
                num = functools.reduce(jnp.add, [wi * ref[hp, rows, :] for wi, ref in zip(w, acc_refs)])
                out_ref[0, rows, hp * LANES:(hp + 1) * LANES] = (num / functools.reduce(jnp.add, w)).astype(BF16)
                return carry
            lax.fori_loop(0, ATT_SPAN // chunk, body, 0)


def _attention(qa, ka, va, batch, seq):
    n_res = DILATIONS[-1]
    n_span = seq // ATT_SPAN
    operands, in_specs = [], []
    for d in DILATIONS:
        per = n_res // d
        views = [x.reshape(batch, seq // d, d * A_WIDTH) for x in (qa, ka, va)]
        blk = (1, ATT_BLK, A_WIDTH)

        def cur(b, n, r, d=d, per=per):
            return (b, n * per + r // d, r % d)

        def prev(b, n, r, d=d, per=per):
            return (b, jnp.maximum(n * per + r // d - 1, 0), r % d)

        operands += [views[0], views[1], views[1], views[2], views[2]]
        in_specs += [pl.BlockSpec(blk, cur), pl.BlockSpec(blk, prev), pl.BlockSpec(blk, cur),
                     pl.BlockSpec(blk, prev), pl.BlockSpec(blk, cur)]
    out = pl.pallas_call(
        _attn_kernel,
        grid=(batch, n_span, n_res),
        in_specs=in_specs,
        out_specs=pl.BlockSpec((1, ATT_SPAN, A_WIDTH), lambda b, n, r: (b, n, 0)),
        out_shape=jax.ShapeDtypeStruct((batch, seq, A_WIDTH), BF16),
        scratch_shapes=[pltpu.VMEM((A_WIDTH // LANES, ATT_SPAN, LANES), F32)] * (2 * len(DILATIONS)),
        compiler_params=_params(("arbitrary", "arbitrary", "arbitrary")),
        name="dilated_attention",
    )(*operands)
    return out.reshape(batch * seq, A_WIDTH)


def _gla_kernel(q_ref, k_ref, la_ref, v_ref, r_ref, g_ref, o_ref, state_ref, *, n_chunks):
    @pl.when(pl.program_id(2) == 0)
    def _():
        state_ref[...] = jnp.zeros_like(state_ref)

    c = GLA_CHUNK
    row = lax.broadcasted_iota(jnp.int32, (c, c), 0)
    col = lax.broadcasted_iota(jnp.int32, (c, c), 1)
    tril = col <= row
    tril_ones = jnp.where(tril, 1.0, 0.0).astype(BF16)
    lane = lax.broadcasted_iota(jnp.int32, (c, LANES), 1)
    lane_sq = lax.broadcasted_iota(jnp.int32, (DV_B, LANES), 1)
    gain = g_ref[...]
    for ci in range(n_chunks):
        rows = slice(ci * c, (ci + 1) * c)
        la = la_ref[rows, :]
        hi = la.astype(BF16)
        rem = la - hi.astype(F32)
        mid = rem.astype(BF16)
        lo = (rem - mid.astype(F32)).astype(BF16)
        b = _dot(tril_ones, hi) + _dot(tril_ones, mid) + _dot(tril_ones, lo)
        b_last = b[c - 1:c, :]
        b_ref = b[c // 2 - 1:c // 2, :]
        q = q_ref[rows, :] * (DK_B ** -0.5)
        k = k_ref[rows, :]
        q_intra = q * jnp.exp(b - b_ref)
        k_intra = (k * jnp.exp(b_ref - b)).astype(BF16)
        k_decay = (k * jnp.exp(b_last - b)).astype(BF16)
        q_inter = (q * jnp.exp(b)).astype(BF16)
        decay = jnp.exp(b_last)
        for h in range(2):
            head_lanes = (lane // DK_B) == h
            v = v_ref[rows, h * DV_B:(h + 1) * DV_B]
            att = _dot_nt(jnp.where(head_lanes, q_intra, 0.0).astype(BF16), k_intra)
            att = jnp.where(tril, att, 0.0)
            state = state_ref[h]
            o = _dot(att.astype(BF16), v) + _dot_nt(q_inter, state.astype(BF16))
            v_t = v.astype(F32).T.astype(BF16)
            kv = jnp.where((lane_sq // DK_B) == h, _dot(v_t, k_decay), 0.0)
            state_ref[h] = state * decay + kv
            o = _rms(o) * gain
            rg = r_ref[rows, h * DV_B:(h + 1) * DV_B]
            o_ref[rows, h * DV_B:(h + 1) * DV_B] = (o * (rg * jax.nn.sigmoid(rg))).astype(BF16)


def _gla(qb, kb, la, vb, rb, g_out, batch, seq, tg=512):
    t = batch * seq
    n_pairs = N_HEADS_B // 2
    per = seq // tg
    qk = pl.BlockSpec((tg, LANES), lambda b, p, i: (b * per + i, p))
    vr = pl.BlockSpec((tg, 2 * DV_B), lambda b, p, i: (b * per + i, p))
    return pl.pallas_call(
        functools.partial(_gla_kernel, n_chunks=tg // GLA_CHUNK),
        grid=(batch, n_pairs, per),
        in_specs=[qk, qk, qk, vr, vr, _const_spec((1, DV_B))],
        out_specs=vr,
        out_shape=jax.ShapeDtypeStruct((t, B_V), BF16),
        scratch_shapes=[pltpu.VMEM((2, DV_B, LANES), F32)],
        compiler_params=_params(("arbitrary", "arbitrary", "arbitrary")),
        name="gla",
    )(qb, kb, la, vb, rb, g_out)


def _sgu_kernel(h_ref, g_ref, w_in_ref, g_sgu_ref, ws_ref, bias_ref, o1_ref, o2_ref, hn_ref, v_ref, *, n_chunks):
    d = D_MODEL
    hn_ref[...] = (_rms(h_ref[...]) * g_ref[...]).astype(BF16)
    v = jax.nn.gelu(_dot(hn_ref[...], w_in_ref[:, d:]))
    v_ref[...] = (_rms(v) * g_sgu_ref[...]).astype(BF16)
    row = lax.broadcasted_iota(jnp.int32, (SGU_CHUNK, SGU_CHUNK), 0)
    col = lax.broadcasted_iota(jnp.int32, (SGU_CHUNK, SGU_CHUNK), 1)
    tril = col <= row
    half = d // 2
    for g in range(N_GROUPS_C):
        ws = jnp.where(tril, ws_ref[g], 0.0).astype(BF16)
        cols = slice(g * SGU_CHUNK, (g + 1) * SGU_CHUNK)
        u_all = jax.nn.gelu(_dot(hn_ref[...], w_in_ref[:, g * SGU_CHUNK:(g + 1) * SGU_CHUNK]))
        for ci in range(n_chunks):
            rows = slice(ci * SGU_CHUNK, (ci + 1) * SGU_CHUNK)
            sv = _dot(ws, v_ref[rows, cols]) + bias_ref[:, cols]
            y = (u_all[ci * SGU_CHUNK:(ci + 1) * SGU_CHUNK] * sv).astype(BF16)
            if g < N_GROUPS_C // 2:
                o1_ref[rows, cols] = y
            else:
                o2_ref[rows, g * SGU_CHUNK - half:(g + 1) * SGU_CHUNK - half] = y


def _sgu(h, g, w_in, g_sgu, ws, bias, tm=256):
    t = h.shape[0]
    row = lambda n: pl.BlockSpec((tm, n), lambda i: (i, 0))
    half = D_MODEL // 2
    return pl.pallas_call(
        functools.partial(_sgu_kernel, n_chunks=tm // SGU_CHUNK),
        grid=(t // tm,),
        in_specs=[row(D_MODEL), _const_spec((1, D_MODEL)), _const_spec(w_in.shape),
                  _const_spec((1, D_MODEL)), _const_spec(ws.shape), _const_spec(bias.shape)],
        out_specs=[row(half), row(half)],
        out_shape=[jax.ShapeDtypeStruct((t, half), BF16)] * 2,
        scratch_shapes=[pltpu.VMEM((tm, D_MODEL), BF16), pltpu.VMEM((tm, D_MODEL), BF16)],
        compiler_params=_params(("arbitrary",)),
        name="sgu",
    )(h, g, w_in, g_sgu, ws, bias)


def _shift_rows(z, prev, k):
    rolled = pltpu.roll(z, k, axis=0)
    head = jnp.where(lax.broadcasted_iota(jnp.int32, prev.shape, 0) < k,
                     pltpu.roll(prev, k, axis=0), rolled[:8])
    return jnp.concatenate([head, rolled[8:]], axis=0)


def _ffn_kernel(h_ref, o1_ref, o2_ref, w_o_ref, g_ref, w_up_ref, cw_ref, cb_ref, w_down_ref, gf_ref,
                out_ref, hn_ref, act_ref, carry_ref, *, final_norm):
    @pl.when(pl.program_id(1) == 0)
    def _():
        carry_ref[...] = jnp.zeros_like(carry_ref)

    tm = h_ref.shape[0]
    o = jnp.concatenate([o1_ref[...], o2_ref[...]], axis=1)
    h1 = h_ref[...] + _dot(o, w_o_ref[...])
    out_ref[...] = h1
    hn_ref[...] = (_rms(h1) * g_ref[...]).astype(BF16)

    def conv(idx, off):
        z = _dot(hn_ref[...], w_up_ref[:, off:off + FF_CHUNK])
        prev = carry_ref[idx]
        carry_ref[idx] = z[tm - 8:]
        w = cw_ref[:, off:off + FF_CHUNK]
        acc = _shift_rows(z, prev, 2) * w[0:1] + _shift_rows(z, prev, 1) * w[1:2] + z * w[2:3]
        return acc + cb_ref[:, off:off + FF_CHUNK]

    for c in range(N_FF_CHUNKS):
        gate = conv(c, c * FF_CHUNK)
        up = conv(N_FF_CHUNKS + c, D_FF + c * FF_CHUNK)
        act_ref[:, c * FF_CHUNK:(c + 1) * FF_CHUNK] = (gate * jax.nn.sigmoid(gate) * up).astype(BF16)
    h2 = out_ref[...] + _dot(act_ref[...], w_down_ref[...])
    if final_norm:
        h2 = _rms(h2) * gf_ref[...]
    out_ref[...] = h2


def _ffn(h, o1, o2, w_o, g, w_up, conv_w, conv_b, w_down, g_final, batch, seq, final_norm, tm=512):
    t = batch * seq
    per = seq // tm
    row = lambda n: pl.BlockSpec((tm, n), lambda b, i: (b * per + i, 0))
    return pl.pallas_call(
        functools.partial(_ffn_kernel, final_norm=final_norm),
        grid=(batch, per),
        in_specs=[row(D_MODEL), row(D_MODEL // 2), row(D_MODEL // 2), _const_spec(w_o.shape),
                  _const_spec((1, D_MODEL)), _const_spec(w_up.shape), _const_spec(conv_w.shape),
                  _const_spec(conv_b.shape), _const_spec(w_down.shape), _const_spec((1, D_MODEL))],
        out_specs=row(D_MODEL),
        out_shape=jax.ShapeDtypeStruct((t, D_MODEL), F32),
        scratch_shapes=[pltpu.VMEM((tm, D_MODEL), BF16), pltpu.VMEM((tm, D_FF), BF16),
                        pltpu.VMEM((2 * N_FF_CHUNKS, 8, FF_CHUNK), F32)],
        compiler_params=_params(("arbitrary", "arbitrary")),
        name="ffn",
    )(h, o1, o2, w_o, g, w_up, conv_w, conv_b, w_down, g_final)


def kernel(x, positions, norm_mix, norm_ffn, w_in_ab, w_gate_up, b_gate_up, g_out_b, w_out_ab, w_in_c,
           g_sgu, w_spatial, b_spatial, w_out_c, w_up, conv_w, conv_b, w_down, norm_final):
    batch, seq, d = x.shape
    depth = norm_mix.shape[0]
    t = batch * seq
    h = x.reshape(t, d)
    cos, sin = _rope_tables(positions)
    row = lambda v: v.reshape(1, -1)
    qkv_a = 3 * A_WIDTH
    b_end = qkv_a + 2 * B_QK + 2 * B_V
    for layer in range(depth):
        i = layer // 2
        if layer % 2 == 0:
            w_in = w_in_ab[i]
            wa = w_in[:, :qkv_a].astype(BF16)
            wb = w_in[:, qkv_a:b_end].astype(BF16)
            wg = jnp.pad(w_in[:, b_end:], ((0, 0), (0, LANES - GATE_RANK))).astype(BF16)
            wgu = jnp.pad(w_gate_up[i], ((0, LANES - GATE_RANK), (0, 0))).astype(BF16)
            qa, ka, va, qb, kb, vb, rb, la = _proj_ab(
                h, row(norm_mix[layer]), wa, wb, wg, wgu, row(b_gate_up[i]), cos, sin)
            o1 = _attention(qa, ka, va, batch, seq)
            o2 = _gla(qb, kb, la, vb, rb, row(g_out_b[i]), batch, seq)
            w_o = w_out_ab[i]
        else:
            bias = jnp.repeat(b_spatial[i].T, SGU_CHUNK, axis=1)
            o1, o2 = _sgu(h, row(norm_mix[layer]), w_in_c[i].astype(BF16), row(g_sgu[i]), w_spatial[i], bias)
            w_o = w_out_c[i]
        h = _ffn(h, o1, o2, w_o.astype(BF16), row(norm_ffn[layer]), w_up[layer].astype(BF16), conv_w[layer],
                 row(conv_b[layer]), w_down[layer].astype(BF16), row(norm_final), batch, seq,
                 final_norm=(layer == depth - 1))
    return h.reshape(batch, seq, d)
```

```python
import functools

import numpy as np
import jax
import jax.numpy as jnp
from jax import lax
from jax.experimental import pallas as pl
from jax.experimental.pallas import tpu as pltpu

F32 = jnp.float32
BF16 = jnp.bfloat16

D_MODEL = 1024
N_HEADS_A = 8
HEAD_DIM_A = 64
ROT_DIM = HEAD_DIM_A // 4
ROPE_THETA = 500000.0
DILATIONS = (1, 4, 16)
ATT_BLK = 128
ATT_SPAN = ATT_BLK * DILATIONS[-1]
N_HEADS_B = 4
DV_B = 128
DK_B = 64
GATE_RANK = 16
GATE_TAU = 16.0
GLA_CHUNK = 64
N_GROUPS_C = 8
SGU_CHUNK = 128
D_FF = 2816
CONV_W = 3
A_WIDTH = N_HEADS_A * HEAD_DIM_A
B_QK = N_HEADS_B * DK_B
B_V = N_HEADS_B * DV_B
NEG = -1e30
EPS = 1e-6
LANES = 128
FF_CHUNK = 256
N_FF_CHUNKS = D_FF // FF_CHUNK

VMEM_LIMIT = 56 * 1024 * 1024


def _dot(a, b):
    return jnp.dot(a, b, preferred_element_type=F32)


def _dot_nt(a, b):
    return lax.dot_general(a, b, (((1,), (1,)), ((), ())), preferred_element_type=F32)


def _rms(x):
    return x * lax.rsqrt(jnp.mean(x * x, axis=-1, keepdims=True) + EPS)


def _const_spec(shape):
    nd = len(shape)
    return pl.BlockSpec(shape, lambda *_: (0,) * nd, pipeline_mode=pl.Buffered(1))


def _params(sem, vmem=VMEM_LIMIT):
    return pltpu.CompilerParams(dimension_semantics=sem, vmem_limit_bytes=vmem)


def _rope_tab_kernel(pos_ref, inv_ref, sgn_ref, c_ref, s_ref):
    ang = pos_ref[...].astype(F32) * inv_ref[...]
    c_ref[...] = jnp.cos(ang)
    s_ref[...] = sgn_ref[...] * jnp.sin(ang)


def _rope_tables(positions):
    t = positions.size
    half = ROT_DIM // 2
    inv = np.float64(ROPE_THETA) ** (-np.arange(half, dtype=np.float64) * (2.0 / ROT_DIM))
    dim = np.arange(LANES) % HEAD_DIM_A
    inv_lane = np.where(dim < ROT_DIM, inv[dim % half], 0.0).astype(np.float32)[None, :]
    sgn_lane = np.where(dim < half, -1.0, np.where(dim < ROT_DIM, 1.0, 0.0)).astype(np.float32)[None, :]
    tm = 1024
    pos = positions.reshape(t, 1)
    return pl.pallas_call(
        _rope_tab_kernel,
        grid=(t // tm,),
        in_specs=[pl.BlockSpec((tm, 1), lambda i: (i, 0)),
                  _const_spec((1, LANES)), _const_spec((1, LANES))],
        out_specs=[pl.BlockSpec((tm, LANES), lambda i: (i, 0))] * 2,
        out_shape=[jax.ShapeDtypeStruct((t, LANES), F32)] * 2,
        compiler_params=_params(("arbitrary",)),
        name="rope_tables",
    )(pos, jnp.asarray(inv_lane), jnp.asarray(sgn_lane))


def _proj_ab_kernel(h_ref, g_ref, wa_ref, wb_ref, wg_ref, wgu_ref, bgu_ref, c_ref, s_ref,
                    qa_ref, ka_ref, va_ref, qb_ref, kb_ref, vb_ref, rb_ref, la_ref, hn_ref):
    hn_ref[...] = (_rms(h_ref[...]) * g_ref[...]).astype(BF16)
    cos = c_ref[...]
    sin = s_ref[...]
    lane = lax.broadcasted_iota(jnp.int32, cos.shape, 1)
    first_half = (lane % HEAD_DIM_A) < (ROT_DIM // 2)

    def rope(x):
        partner = jnp.where(first_half,
                            pltpu.roll(x, LANES - ROT_DIM // 2, axis=1),
                            pltpu.roll(x, ROT_DIM // 2, axis=1))
        return x * cos + partner * sin

    for j in range(A_WIDTH // LANES):
        sl = slice(j * LANES, (j + 1) * LANES)
        q = _dot(hn_ref[...], wa_ref[:, j * LANES:(j + 1) * LANES])
        qa_ref[:, sl] = rope(q).astype(BF16)
        k = _dot(hn_ref[...], wa_ref[:, A_WIDTH + j * LANES:A_WIDTH + (j + 1) * LANES])
        ka_ref[:, sl] = rope(k).astype(BF16)
    va_ref[...] = _dot(hn_ref[...], wa_ref[:, 2 * A_WIDTH:]).astype(BF16)
    qb_ref[...] = _dot(hn_ref[...], wb_ref[:, :B_QK])
    kb_ref[...] = _dot(hn_ref[...], wb_ref[:, B_QK:2 * B_QK])
    vb_ref[...] = _dot(hn_ref[...], wb_ref[:, 2 * B_QK:2 * B_QK + B_V]).astype(BF16)
    rb_ref[...] = _dot(hn_ref[...], wb_ref[:, 2 * B_QK + B_V:])
    gl = _dot(hn_ref[...], wg_ref[...])
    g = _dot(gl.astype(BF16), wgu_ref[...]) + bgu_ref[...]
    la_ref[...] = (jnp.minimum(g, 0.0) - jnp.log1p(jnp.exp(-jnp.abs(g)))) * (1.0 / GATE_TAU)


def _proj_ab(h, g, wa, wb, wg, wgu, bgu, cos, sin, tm=512):
    t = h.shape[0]
    row = lambda n: pl.BlockSpec((tm, n), lambda i: (i, 0))
    out_widths = (A_WIDTH, A_WIDTH, A_WIDTH, B_QK, B_QK, B_V, B_V, B_QK)
    out_dtypes = (BF16, BF16, BF16, F32, F32, BF16, F32, F32)
    return pl.pallas_call(
        _proj_ab_kernel,
        grid=(t // tm,),
        in_specs=[row(D_MODEL), _const_spec((1, D_MODEL)), _const_spec(wa.shape), _const_spec(wb.shape),
                  _const_spec(wg.shape), _const_spec(wgu.shape), _const_spec(bgu.shape),
                  row(LANES), row(LANES)],
        out_specs=[row(n) for n in out_widths],
        out_shape=[jax.ShapeDtypeStruct((t, n), d) for n, d in zip(out_widths, out_dtypes)],
        scratch_shapes=[pltpu.VMEM((tm, D_MODEL), BF16)],
        compiler_params=_params(("arbitrary",)),
        name="proj_ab",
    )(h, g, wa, wb, wg, wgu, bgu, cos, sin)


N_PAIRS_A = A_WIDTH // LANES


def _attn_bias():
    a = np.arange(ATT_BLK)[:, None]
    c = np.arange(2 * ATT_BLK)[None, :]
    band = (c >= a) & (c <= a + ATT_BLK)
    first = band & (c >= ATT_BLK)
    return np.where(np.stack([band, first]), 0.0, NEG).astype(np.float32)


def _attn_kernel(*refs):
    nb = len(DILATIONS)
    in_refs = refs[:5 * nb]
    bias_ref = refs[5 * nb]
    out_ref = refs[5 * nb + 1]
    acc_refs = refs[5 * nb + 2:5 * nb + 2 + nb]
    lse_refs = refs[5 * nb + 2 + nb:5 * nb + 2 + 2 * nb]
    s_ref, p_ref, inv_ref, stat_ref = refs[5 * nb + 2 + 2 * nb:]
    n = pl.program_id(1)
    r = pl.program_id(2)
    n_res = DILATIONS[-1]
    lane = lax.broadcasted_iota(jnp.int32, (ATT_BLK, LANES), 1)
    low_head = lane < HEAD_DIM_A
    zero = jnp.zeros((), BF16)
    scale = jnp.asarray(HEAD_DIM_A ** -0.5, BF16)

    for g, d in enumerate(DILATIONS):
        q_ref, kp_ref, kc_ref = in_refs[5 * g:5 * g + 3]
        blk = n * (n_res // d) + r // d
        bias = bias_ref[jnp.where(blk == 0, 1, 0)]
        for hp in range(N_PAIRS_A):
            sl = slice(hp * LANES, (hp + 1) * LANES)
            q2 = q_ref[0, :, sl] * scale
            kcat = jnp.concatenate([kp_ref[0, :, sl], kc_ref[0, :, sl]], axis=0)
            for hh in range(2):
                qm = jnp.where(low_head == (hh == 0), q2, zero)
                s_ref[(g * N_PAIRS_A + hp) * 2 + hh] = _dot_nt(qm, kcat) + bias

    for u in range(nb * N_PAIRS_A * 2):
        s = s_ref[u]
        m = jnp.max(s, axis=-1, keepdims=True)
        p = jnp.exp(s - m)
        l = jnp.sum(p, axis=-1, keepdims=True)
        p_ref[u] = p.astype(BF16)
        inv_ref[u] = jnp.broadcast_to(1.0 / l, (ATT_BLK, LANES))
        stat_ref[u] = jnp.broadcast_to(m + jnp.log(l), (ATT_BLK, LANES))

    for g, d in enumerate(DILATIONS):
        vp_ref, vc_ref = in_refs[5 * g + 3:5 * g + 5]
        if d == 1:
            rows = pl.ds(pl.multiple_of(r * ATT_BLK, ATT_BLK), ATT_BLK)
        else:
            rows = pl.ds((r // d) * (ATT_BLK * d) + r % d, ATT_BLK, stride=d)
        for hp in range(N_PAIRS_A):
            sl = slice(hp * LANES, (hp + 1) * LANES)
            vcat = jnp.concatenate([vp_ref[0, :, sl], vc_ref[0, :, sl]], axis=0)
            u0 = (g * N_PAIRS_A + hp) * 2
            o0 = _dot(p_ref[u0], vcat) * inv_ref[u0]
            o1 = _dot(p_ref[u0 + 1], vcat) * inv_ref[u0 + 1]
            acc_refs[g][hp, rows, :] = jnp.where(low_head, o0, o1)
            lse_refs[g][hp, rows, :] = jnp.where(low_head, stat_ref[u0], stat_ref[u0 + 1])

    @pl.when(r == n_res - 1)
    def _():
        chunk = 256
        for hp in range(A_WIDTH // LANES):
            def body(i, carry):
                rows = pl.ds(pl.multiple_of(i * chunk, chunk), chunk)
                lse = [ref[hp, rows, :] for ref in lse_refs]
                top = functools.reduce(jnp.maximum, lse)
                w = [jnp.exp(x - top) for x in lse]
                num = functools.reduce(jnp.add, [wi * ref[hp, rows, :] for wi, ref in zip(w, acc_refs)])
                out_ref[0, rows, hp * LANES:(hp + 1) * LANES] = (num / functools.reduce(jnp.add, w)).astype(BF16)
                return carry
            lax.fori_loop(0, ATT_SPAN // chunk, body, 0)


def _attention(qa, ka, va, batch, seq):
    n_res = DILATIONS[-1]
    n_span = seq // ATT_SPAN
    operands, in_specs = [], []
    for d in DILATIONS:
        per = n_res // d
        views = [x.reshape(batch, seq // d, d * A_WIDTH) for x in (qa, ka, va)]
        blk = (1, ATT_BLK, A_WIDTH)

        def cur(b, n, r, d=d, per=per):
            return (b, n * per + r // d, r % d)

        def prev(b, n, r, d=d, per=per):
            return (b, jnp.maximum(n * per + r // d - 1, 0), r % d)

        operands += [views[0], views[1], views[1], views[2], views[2]]
        in_specs += [pl.BlockSpec(blk, cur), pl.BlockSpec(blk, prev), pl.BlockSpec(blk, cur),
                     pl.BlockSpec(blk, prev), pl.BlockSpec(blk, cur)]
    bias = jnp.asarray(_attn_bias())
    n_units = len(DILATIONS) * N_HEADS_A
    out = pl.pallas_call(
        _attn_kernel,
        grid=(batch, n_span, n_res),
        in_specs=in_specs + [_const_spec(bias.shape)],
        out_specs=pl.BlockSpec((1, ATT_SPAN, A_WIDTH), lambda b, n, r: (b, n, 0)),
        out_shape=jax.ShapeDtypeStruct((batch, seq, A_WIDTH), BF16),
        scratch_shapes=[pltpu.VMEM((N_PAIRS_A, ATT_SPAN, LANES), F32)] * (2 * len(DILATIONS))
        + [pltpu.VMEM((n_units, ATT_BLK, 2 * ATT_BLK), F32), pltpu.VMEM((n_units, ATT_BLK, 2 * ATT_BLK), BF16),
           pltpu.VMEM((n_units, ATT_BLK, LANES), F32), pltpu.VMEM((n_units, ATT_BLK, LANES), F32)],
        compiler_params=_params(("arbitrary", "arbitrary", "arbitrary")),
        name="dilated_attention",
    )(*operands, bias)
    return out.reshape(batch * seq, A_WIDTH)


def _gla_kernel(q_ref, k_ref, la_ref, v_ref, r_ref, g_ref, o_ref, state_ref, *, n_chunks):
    @pl.when(pl.program_id(2) == 0)
    def _():
        state_ref[...] = jnp.zeros_like(state_ref)

    c = GLA_CHUNK
    row = lax.broadcasted_iota(jnp.int32, (c, c), 0)
    col = lax.broadcasted_iota(jnp.int32, (c, c), 1)
    tril = col <= row
    tril_ones = jnp.where(tril, 1.0, 0.0).astype(BF16)
    lane = lax.broadcasted_iota(jnp.int32, (c, LANES), 1)
    lane_sq = lax.broadcasted_iota(jnp.int32, (DV_B, LANES), 1)
    gain = g_ref[...]
    for ci in range(n_chunks):
        rows = slice(ci * c, (ci + 1) * c)
        la = la_ref[rows, :]
        hi = la.astype(BF16)
        rem = la - hi.astype(F32)
        mid = rem.astype(BF16)
        lo = (rem - mid.astype(F32)).astype(BF16)
        b = _dot(tril_ones, hi) + _dot(tril_ones, mid) + _dot(tril_ones, lo)
        b_last = b[c - 1:c, :]
        b_ref = b[c // 2 - 1:c // 2, :]
        q = q_ref[rows, :] * (DK_B ** -0.5)
        k = k_ref[rows, :]
        q_intra = q * jnp.exp(b - b_ref)
        k_intra = (k * jnp.exp(b_ref - b)).astype(BF16)
        k_decay = (k * jnp.exp(b_last - b)).astype(BF16)
        q_inter = (q * jnp.exp(b)).astype(BF16)
        decay = jnp.exp(b_last)
        for h in range(2):
            head_lanes = (lane // DK_B) == h
            v = v_ref[rows, h * DV_B:(h + 1) * DV_B]
            att = _dot_nt(jnp.where(head_lanes, q_intra, 0.0).astype(BF16), k_intra)
            att = jnp.where(tril, att, 0.0)
            state = state_ref[h]
            o = _dot(att.astype(BF16), v) + _dot_nt(q_inter, state.astype(BF16))
            v_t = v.astype(F32).T.astype(BF16)
            kv = jnp.where((lane_sq // DK_B) == h, _dot(v_t, k_decay), 0.0)
            state_ref[h] = state * decay + kv
            o = _rms(o) * gain
            rg = r_ref[rows, h * DV_B:(h + 1) * DV_B]
            o_ref[rows, h * DV_B:(h + 1) * DV_B] = (o * (rg * jax.nn.sigmoid(rg))).astype(BF16)


def _gla(qb, kb, la, vb, rb, g_out, batch, seq, tg=512):
    t = batch * seq
    n_pairs = N_HEADS_B // 2
    per = seq // tg
    qk = pl.BlockSpec((tg, LANES), lambda b, p, i: (b * per + i, p))
    vr = pl.BlockSpec((tg, 2 * DV_B), lambda b, p, i: (b * per + i, p))
    return pl.pallas_call(
        functools.partial(_gla_kernel, n_chunks=tg // GLA_CHUNK),
        grid=(batch, n_pairs, per),
        in_specs=[qk, qk, qk, vr, vr, _const_spec((1, DV_B))],
        out_specs=vr,
        out_shape=jax.ShapeDtypeStruct((t, B_V), BF16),
        scratch_shapes=[pltpu.VMEM((2, DV_B, LANES), F32)],
        compiler_params=_params(("arbitrary", "arbitrary", "arbitrary")),
        name="gla",
    )(qb, kb, la, vb, rb, g_out)


def _sgu_kernel(h_ref, g_ref, w_in_ref, g_sgu_ref, ws_ref, bias_ref, o1_ref, o2_ref, hn_ref, v_ref, *, n_chunks):
    d = D_MODEL
    hn_ref[...] = (_rms(h_ref[...]) * g_ref[...]).astype(BF16)
    v = jax.nn.gelu(_dot(hn_ref[...], w_in_ref[:, d:]))
    v_ref[...] = (_rms(v) * g_sgu_ref[...]).astype(BF16)
    row = lax.broadcasted_iota(jnp.int32, (SGU_CHUNK, SGU_CHUNK), 0)
    col = lax.broadcasted_iota(jnp.int32, (SGU_CHUNK, SGU_CHUNK), 1)
    tril = col <= row
    half = d // 2
    for g in range(N_GROUPS_C):
        ws = jnp.where(tril, ws_ref[g], 0.0).astype(BF16)
        cols = slice(g * SGU_CHUNK, (g + 1) * SGU_CHUNK)
        u_all = jax.nn.gelu(_dot(hn_ref[...], w_in_ref[:, g * SGU_CHUNK:(g + 1) * SGU_CHUNK]))
        for ci in range(n_chunks):
            rows = slice(ci * SGU_CHUNK, (ci + 1) * SGU_CHUNK)
            sv = _dot(ws, v_ref[rows, cols]) + bias_ref[:, cols]
            y = (u_all[ci * SGU_CHUNK:(ci + 1) * SGU_CHUNK] * sv).astype(BF16)
            if g < N_GROUPS_C // 2:
                o1_ref[rows, cols] = y
            else:
                o2_ref[rows, g * SGU_CHUNK - half:(g + 1) * SGU_CHUNK - half] = y


def _sgu(h, g, w_in, g_sgu, ws, bias, tm=256):
    t = h.shape[0]
    row = lambda n: pl.BlockSpec((tm, n), lambda i: (i, 0))
    half = D_MODEL // 2
    return pl.pallas_call(
        functools.partial(_sgu_kernel, n_chunks=tm // SGU_CHUNK),
        grid=(t // tm,),
        in_specs=[row(D_MODEL), _const_spec((1, D_MODEL)), _const_spec(w_in.shape),
                  _const_spec((1, D_MODEL)), _const_spec(ws.shape), _const_spec(bias.shape)],
        out_specs=[row(half), row(half)],
        out_shape=[jax.ShapeDtypeStruct((t, half), BF16)] * 2,
        scratch_shapes=[pltpu.VMEM((tm, D_MODEL), BF16), pltpu.VMEM((tm, D_MODEL), BF16)],
        compiler_params=_params(("arbitrary",)),
        name="sgu",
    )(h, g, w_in, g_sgu, ws, bias)


def _shift_rows(z, prev, k):
    rolled = pltpu.roll(z, k, axis=0)
    head = jnp.where(lax.broadcasted_iota(jnp.int32, prev.shape, 0) < k,
                     pltpu.roll(prev, k, axis=0), rolled[:8])
    return jnp.concatenate([head, rolled[8:]], axis=0)


def _ffn_kernel(h_ref, o1_ref, o2_ref, w_o_ref, g_ref, w_up_ref, cw_ref, cb_ref, w_down_ref, gf_ref,
                out_ref, hn_ref, act_ref, carry_ref, *, final_norm):
    @pl.when(pl.program_id(1) == 0)
    def _():
        carry_ref[...] = jnp.zeros_like(carry_ref)

    tm = h_ref.shape[0]
    o = jnp.concatenate([o1_ref[...], o2_ref[...]], axis=1)
    h1 = h_ref[...] + _dot(o, w_o_ref[...])
    out_ref[...] = h1
    hn_ref[...] = (_rms(h1) * g_ref[...]).astype(BF16)

    def conv(idx, off):
        z = _dot(hn_ref[...], w_up_ref[:, off:off + FF_CHUNK])
        prev = carry_ref[idx]
        carry_ref[idx] = z[tm - 8:]
        w = cw_ref[:, off:off + FF_CHUNK]
        acc = _shift_rows(z, prev, 2) * w[0:1] + _shift_rows(z, prev, 1) * w[1:2] + z * w[2:3]
        return acc + cb_ref[:, off:off + FF_CHUNK]

    for c in range(N_FF_CHUNKS):
        gate = conv(c, c * FF_CHUNK)
        up = conv(N_FF_CHUNKS + c, D_FF + c * FF_CHUNK)
        act_ref[:, c * FF_CHUNK:(c + 1) * FF_CHUNK] = (gate * jax.nn.sigmoid(gate) * up).astype(BF16)
    h2 = out_ref[...] + _dot(act_ref[...], w_down_ref[...])
    if final_norm:
        h2 = _rms(h2) * gf_ref[...]
    out_ref[...] = h2


def _ffn(h, o1, o2, w_o, g, w_up, conv_w, conv_b, w_down, g_final, batch, seq, final_norm, tm=512):
    t = batch * seq
    per = seq // tm
    row = lambda n: pl.BlockSpec((tm, n), lambda b, i: (b * per + i, 0))
    return pl.pallas_call(
        functools.partial(_ffn_kernel, final_norm=final_norm),
        grid=(batch, per),
        in_specs=[row(D_MODEL), row(D_MODEL // 2), row(D_MODEL // 2), _const_spec(w_o.shape),
                  _const_spec((1, D_MODEL)), _const_spec(w_up.shape), _const_spec(conv_w.shape),
                  _const_spec(conv_b.shape), _const_spec(w_down.shape), _const_spec((1, D_MODEL))],
        out_specs=row(D_MODEL),
        out_shape=jax.ShapeDtypeStruct((t, D_MODEL), F32),
        scratch_shapes=[pltpu.VMEM((tm, D_MODEL), BF16), pltpu.VMEM((tm, D_FF), BF16),
                        pltpu.VMEM((2 * N_FF_CHUNKS, 8, FF_CHUNK), F32)],
        compiler_params=_params(("arbitrary", "arbitrary")),
        name="ffn",
    )(h, o1, o2, w_o, g, w_up, conv_w, conv_b, w_down, g_final)


def kernel(x, positions, norm_mix, norm_ffn, w_in_ab, w_gate_up, b_gate_up, g_out_b, w_out_ab, w_in_c,
           g_sgu, w_spatial, b_spatial, w_out_c, w_up, conv_w, conv_b, w_down, norm_final):
    batch, seq, d = x.shape
    depth = norm_mix.shape[0]
    t = batch * seq
    h = x.reshape(t, d)
    cos, sin = _rope_tables(positions)
    row = lambda v: v.reshape(1, -1)
    qkv_a = 3 * A_WIDTH
    b_end = qkv_a + 2 * B_QK + 2 * B_V
    for layer in range(depth):
        i = layer // 2
        if layer % 2 == 0:
            w_in = w_in_ab[i]
            wa = w_in[:, :qkv_a].astype(BF16)
            wb = w_in[:, qkv_a:b_end].astype(BF16)
            wg = jnp.pad(w_in[:, b_end:], ((0, 0), (0, LANES - GATE_RANK))).astype(BF16)
            wgu = jnp.pad(w_gate_up[i], ((0, LANES - GATE_RANK), (0, 0))).astype(BF16)
            qa, ka, va, qb, kb, vb, rb, la = _proj_ab(
                h, row(norm_mix[layer]), wa, wb, wg, wgu, row(b_gate_up[i]), cos, sin)
            o1 = _attention(qa, ka, va, batch, seq)
            o2 = _gla(qb, kb, la, vb, rb, row(g_out_b[i]), batch, seq)
            w_o = w_out_ab[i]
        else:
            bias = jnp.repeat(b_spatial[i].T, SGU_CHUNK, axis=1)
            o1, o2 = _sgu(h, row(norm_mix[layer]), w_in_c[i].astype(BF16), row(g_sgu[i]), w_spatial[i], bias)
            w_o = w_out_c[i]
        h = _ffn(h, o1, o2, w_o.astype(BF16), row(norm_ffn[layer]), w_up[layer].astype(BF16), conv_w[layer],
                 row(conv_b[layer]), w_down[layer].astype(BF16), row(norm_final), batch, seq,
                 final_norm=(layer == depth - 1))
    return h.reshape(batch, seq, d)
```

```python
import functools

import numpy as np
import jax
import jax.numpy as jnp
from jax import lax
from jax.experimental import pallas as pl
from jax.experimental.pallas import tpu as pltpu

F32 = jnp.float32
BF16 = jnp.bfloat16

D_MODEL = 1024
N_HEADS_A = 8
HEAD_DIM_A = 64
ROT_DIM = HEAD_DIM_A // 4
ROPE_THETA = 500000.0
DILATIONS = (1, 4, 16)
ATT_BLK = 128
ATT_SPAN = ATT_BLK * DILATIONS[-1]
N_HEADS_B = 4
DV_B = 128
DK_B = 64
GATE_RANK = 16
GATE_TAU = 16.0
GLA_CHUNK = 64
N_GROUPS_C = 8
SGU_CHUNK = 128
D_FF = 2816
CONV_W = 3
A_WIDTH = N_HEADS_A * HEAD_DIM_A
B_QK = N_HEADS_B * DK_B
B_V = N_HEADS_B * DV_B
NEG = -1e30
EPS = 1e-6
LANES = 128
N_PAIRS_A = A_WIDTH // LANES
FF_CHUNK = 256
N_FF_CHUNKS = D_FF // FF_CHUNK

VMEM_LIMIT = 56 * 1024 * 1024


def _dot(a, b):
    return jnp.dot(a, b, preferred_element_type=F32)


def _dot_nt(a, b):
    return lax.dot_general(a, b, (((1,), (1,)), ((), ())), preferred_element_type=F32)


def _rms(x):
    return x * lax.rsqrt(jnp.mean(x * x, axis=-1, keepdims=True) + EPS)


def _const_spec(shape):
    nd = len(shape)
    return pl.BlockSpec(shape, lambda *_: (0,) * nd, pipeline_mode=pl.Buffered(1))


def _params(sem, vmem=VMEM_LIMIT):
    return pltpu.CompilerParams(dimension_semantics=sem, vmem_limit_bytes=vmem)


def _rope_tab_kernel(pos_ref, inv_ref, sgn_ref, c_ref, s_ref):
    ang = pos_ref[...].astype(F32) * inv_ref[...]
    c_ref[...] = jnp.cos(ang)
    s_ref[...] = sgn_ref[...] * jnp.sin(ang)


def _rope_tables(positions):
    t = positions.size
    half = ROT_DIM // 2
    inv = np.float64(ROPE_THETA) ** (-np.arange(half, dtype=np.float64) * (2.0 / ROT_DIM))
    dim = np.arange(LANES) % HEAD_DIM_A
    inv_lane = np.where(dim < ROT_DIM, inv[dim % half], 0.0).astype(np.float32)[None, :]
    sgn_lane = np.where(dim < half, -1.0, np.where(dim < ROT_DIM, 1.0, 0.0)).astype(np.float32)[None, :]
    tm = 1024
    pos = positions.reshape(t, 1)
    return pl.pallas_call(
        _rope_tab_kernel,
        grid=(t // tm,),
        in_specs=[pl.BlockSpec((tm, 1), lambda i: (i, 0)),
                  _const_spec((1, LANES)), _const_spec((1, LANES))],
        out_specs=[pl.BlockSpec((tm, LANES), lambda i: (i, 0))] * 2,
        out_shape=[jax.ShapeDtypeStruct((t, LANES), F32)] * 2,
        compiler_params=_params(("arbitrary",)),
        name="rope_tables",
    )(pos, jnp.asarray(inv_lane), jnp.asarray(sgn_lane))


def _proj_ab_kernel(h_ref, g_ref, wa_ref, wb_ref, wg_ref, wgu_ref, bgu_ref, c_ref, s_ref, *refs):
    nd = len(DILATIONS)
    qkv_refs = refs[:3 * nd]
    qb_ref, kb_ref, vb_ref, rb_ref, la_ref, hn_ref, stage_ref = refs[3 * nd:]
    tm = h_ref.shape[0]
    hn_ref[...] = (_rms(h_ref[...]) * g_ref[...]).astype(BF16)
    cos = c_ref[...]
    sin = s_ref[...]
    lane = lax.broadcasted_iota(jnp.int32, cos.shape, 1)
    first_half = (lane % HEAD_DIM_A) < (ROT_DIM // 2)

    def rope(x):
        partner = jnp.where(first_half,
                            pltpu.roll(x, LANES - ROT_DIM // 2, axis=1),
                            pltpu.roll(x, ROT_DIM // 2, axis=1))
        return x * cos + partner * sin

    for which in range(3):
        for j in range(N_PAIRS_A):
            x = _dot(hn_ref[...], wa_ref[:, which * A_WIDTH + j * LANES:which * A_WIDTH + (j + 1) * LANES])
            if which < 2:
                x = rope(x)
            qkv_refs[which * nd][:, j * LANES:(j + 1) * LANES] = x.astype(BF16)
            stage = stage_ref.at[which * N_PAIRS_A + j]
            stage[...] = x
            for di, d in enumerate(DILATIONS[1:], start=1):
                for r in range(d):
                    piece = stage[pl.ds(r, tm // d, stride=d), :]
                    col = r * A_WIDTH + j * LANES
                    qkv_refs[which * nd + di][:, col:col + LANES] = piece.astype(BF16)
    qb_ref[...] = _dot(hn_ref[...], wb_ref[:, :B_QK])
    kb_ref[...] = _dot(hn_ref[...], wb_ref[:, B_QK:2 * B_QK])
    vb_ref[...] = _dot(hn_ref[...], wb_ref[:, 2 * B_QK:2 * B_QK + B_V]).astype(BF16)
    rb_ref[...] = _dot(hn_ref[...], wb_ref[:, 2 * B_QK + B_V:])
    gl = _dot(hn_ref[...], wg_ref[...])
    g = _dot(gl.astype(BF16), wgu_ref[...]) + bgu_ref[...]
    la_ref[...] = (jnp.minimum(g, 0.0) - jnp.log1p(jnp.exp(-jnp.abs(g)))) * (1.0 / GATE_TAU)


def _proj_ab(h, g, wa, wb, wg, wgu, bgu, cos, sin, tm=512):
    t = h.shape[0]
    row = lambda n: pl.BlockSpec((tm, n), lambda i: (i, 0))
    specs = [pl.BlockSpec((tm // d, d * A_WIDTH), lambda i: (i, 0)) for d in DILATIONS] * 3
    shapes = [jax.ShapeDtypeStruct((t // d, d * A_WIDTH), BF16) for d in DILATIONS] * 3
    out_widths = (B_QK, B_QK, B_V, B_V, B_QK)
    out_dtypes = (F32, F32, BF16, F32, F32)
    outs = pl.pallas_call(
        _proj_ab_kernel,
        grid=(t // tm,),
        in_specs=[row(D_MODEL), _const_spec((1, D_MODEL)), _const_spec(wa.shape), _const_spec(wb.shape),
                  _const_spec(wg.shape), _const_spec(wgu.shape), _const_spec(bgu.shape),
                  row(LANES), row(LANES)],
        out_specs=specs + [row(n) for n in out_widths],
        out_shape=shapes + [jax.ShapeDtypeStruct((t, n), d) for n, d in zip(out_widths, out_dtypes)],
        scratch_shapes=[pltpu.VMEM((tm, D_MODEL), BF16), pltpu.VMEM((3 * N_PAIRS_A, tm, LANES), F32)],
        compiler_params=_params(("arbitrary",)),
        name="proj_ab",
    )(h, g, wa, wb, wg, wgu, bgu, cos, sin)
    return outs[:3 * len(DILATIONS)], outs[3 * len(DILATIONS):]


def _attn_bias():
    a = np.arange(ATT_BLK)[:, None]
    c = np.arange(2 * ATT_BLK)[None, :]
    band = (c >= a) & (c <= a + ATT_BLK)
    first = band & (c >= ATT_BLK)
    return np.where(np.stack([band, first]), 0.0, NEG).astype(np.float32)


def _attn_kernel(*refs):
    nb = len(DILATIONS)
    in_refs = refs[:5 * nb]
    bias_ref = refs[5 * nb]
    out_ref = refs[5 * nb + 1]
    acc_refs = refs[5 * nb + 2:5 * nb + 2 + nb]
    lse_refs = refs[5 * nb + 2 + nb:5 * nb + 2 + 2 * nb]
    s_ref, p_ref, inv_ref, stat_ref = refs[5 * nb + 2 + 2 * nb:]
    n = pl.program_id(1)
    r = pl.program_id(2)
    n_res = DILATIONS[-1]
    lane = lax.broadcasted_iota(jnp.int32, (ATT_BLK, LANES), 1)
    low_head = lane < HEAD_DIM_A
    zero = jnp.zeros((), BF16)
    scale = jnp.asarray(HEAD_DIM_A ** -0.5, BF16)

    for g, d in enumerate(DILATIONS):
        q_ref, kp_ref, kc_ref = in_refs[5 * g:5 * g + 3]
        blk = n * (n_res // d) + r // d
        bias = bias_ref[jnp.where(blk == 0, 1, 0)]
        for hp in range(N_PAIRS_A):
            sl = slice(hp * LANES, (hp + 1) * LANES)
            q2 = q_ref[0, :, sl] * scale
            kcat = jnp.concatenate([kp_ref[0, :, sl], kc_ref[0, :, sl]], axis=0)
            for hh in range(2):
                qm = jnp.where(low_head == (hh == 0), q2, zero)
                s_ref[(g * N_PAIRS_A + hp) * 2 + hh] = _dot_nt(qm, kcat) + bias

    for u in range(nb * N_PAIRS_A * 2):
        s = s_ref[u]
        m = jnp.max(s, axis=-1, keepdims=True)
        p = jnp.exp(s - m)
        l = jnp.sum(p, axis=-1, keepdims=True)
        p_ref[u] = p.astype(BF16)
        inv_ref[u] = jnp.broadcast_to(1.0 / l, (ATT_BLK, LANES))
        stat_ref[u] = jnp.broadcast_to(m + jnp.log(l), (ATT_BLK, LANES))

    for g, d in enumerate(DILATIONS):
        vp_ref, vc_ref = in_refs[5 * g + 3:5 * g + 5]
        if d == 1:
            rows = pl.ds(pl.multiple_of(r * ATT_BLK, ATT_BLK), ATT_BLK)
        else:
            rows = pl.ds((r // d) * (ATT_BLK * d) + r % d, ATT_BLK, stride=d)
        for hp in range(N_PAIRS_A):
            sl = slice(hp * LANES, (hp + 1) * LANES)
            vcat = jnp.concatenate([vp_ref[0, :, sl], vc_ref[0, :, sl]], axis=0)
            u0 = (g * N_PAIRS_A + hp) * 2
            o0 = _dot(p_ref[u0], vcat) * inv_ref[u0]
            o1 = _dot(p_ref[u0 + 1], vcat) * inv_ref[u0 + 1]
            acc_refs[g][hp, rows, :] = jnp.where(low_head, o0, o1)
            lse_refs[g][hp, rows, :] = jnp.where(low_head, stat_ref[u0], stat_ref[u0 + 1])

    @pl.when(r == n_res - 1)
    def _():
        chunk = 256
        for hp in range(A_WIDTH // LANES):
            def body(i, carry):
                rows = pl.ds(pl.multiple_of(i * chunk, chunk), chunk)
                lse = [ref[hp, rows, :] for ref in lse_refs]
                top = functools.reduce(jnp.maximum, lse)
                w = [jnp.exp(x - top) for x in lse]
                num = functools.reduce(jnp.add, [wi * ref[hp, rows, :] for wi, ref in zip(w, acc_refs)])
                out_ref[0, rows, hp * LANES:(hp + 1) * LANES] = (num / functools.reduce(jnp.add, w)).astype(BF16)
                return carry
            lax.fori_loop(0, ATT_SPAN // chunk, body, 0)


def _attention(qkv, batch, seq):
    n_res = DILATIONS[-1]
    n_span = seq // ATT_SPAN
    nd = len(DILATIONS)
    operands, in_specs = [], []
    for di, d in enumerate(DILATIONS):
        per = n_res // d
        views = [qkv[which * nd + di].reshape(batch, seq // d, d * A_WIDTH) for which in range(3)]
        blk = (1, ATT_BLK, A_WIDTH)

        def cur(b, n, r, d=d, per=per):
            return (b, n * per + r // d, r % d)

        def prev(b, n, r, d=d, per=per):
            return (b, jnp.maximum(n * per + r // d - 1, 0), r % d)

        operands += [views[0], views[1], views[1], views[2], views[2]]
        in_specs += [pl.BlockSpec(blk, cur), pl.BlockSpec(blk, prev), pl.BlockSpec(blk, cur),
                     pl.BlockSpec(blk, prev), pl.BlockSpec(blk, cur)]
    bias = jnp.asarray(_attn_bias())
    n_units = len(DILATIONS) * N_HEADS_A
    out = pl.pallas_call(
        _attn_kernel,
        grid=(batch, n_span, n_res),
        in_specs=in_specs + [_const_spec(bias.shape)],
        out_specs=pl.BlockSpec((1, ATT_SPAN, A_WIDTH), lambda b, n, r: (b, n, 0)),
        out_shape=jax.ShapeDtypeStruct((batch, seq, A_WIDTH), BF16),
        scratch_shapes=[pltpu.VMEM((N_PAIRS_A, ATT_SPAN, LANES), F32)] * (2 * len(DILATIONS))
        + [pltpu.VMEM((n_units, ATT_BLK, 2 * ATT_BLK), F32), pltpu.VMEM((n_units, ATT_BLK, 2 * ATT_BLK), BF16),
           pltpu.VMEM((n_units, ATT_BLK, LANES), F32), pltpu.VMEM((n_units, ATT_BLK, LANES), F32)],
        compiler_params=_params(("arbitrary", "arbitrary", "arbitrary")),
        name="dilated_attention",
    )(*operands, bias)
    return out.reshape(batch * seq, A_WIDTH)


def _gla_kernel(q_ref, k_ref, la_ref, v_ref, r_ref, g_ref, o_ref, state_ref,
                qi_ref, ki_ref, kd_ref, qg_ref, dec_ref, kv_ref, sb_ref, oi_ref, *, n_chunks):
    @pl.when(pl.program_id(2) == 0)
    def _():
        state_ref[...] = jnp.zeros_like(state_ref)

    c = GLA_CHUNK
    two = 2 * c
    row2 = lax.broadcasted_iota(jnp.int32, (two, two), 0)
    col2 = lax.broadcasted_iota(jnp.int32, (two, two), 1)
    tril2 = jnp.where((col2 <= row2) & (col2 // c == row2 // c), 1.0, 0.0).astype(BF16)
    gain = g_ref[...]

    for gi in range(n_chunks // 2):
        rows = slice(gi * two, (gi + 1) * two)
        la = la_ref[rows, :]
        hi = la.astype(BF16)
        rem = la - hi.astype(F32)
        mid = rem.astype(BF16)
        lo = (rem - mid.astype(F32)).astype(BF16)
        b3 = _dot(tril2, jnp.concatenate([hi, mid, lo], axis=1))
        b = b3[:, :LANES] + b3[:, LANES:2 * LANES] + b3[:, 2 * LANES:]
        b_last = jnp.concatenate([jnp.broadcast_to(b[c - 1:c], (c, LANES)),
                                  jnp.broadcast_to(b[two - 1:two], (c, LANES))], axis=0)
        b_mid = jnp.concatenate([jnp.broadcast_to(b[c // 2 - 1:c // 2], (c, LANES)),
                                 jnp.broadcast_to(b[c + c // 2 - 1:c + c // 2], (c, LANES))], axis=0)
        q = q_ref[rows, :] * (DK_B ** -0.5)
        k = k_ref[rows, :]
        qi_ref[rows, :] = (q * jnp.exp(b - b_mid)).astype(BF16)
        ki_ref[rows, :] = (k * jnp.exp(b_mid - b)).astype(BF16)
        kd_ref[rows, :] = (k * jnp.exp(b_last - b)).astype(BF16)
        qg_ref[rows, :] = (q * jnp.exp(b)).astype(BF16)
        dec_ref[2 * gi:2 * gi + 1, :] = jnp.exp(b[c - 1:c])
        dec_ref[2 * gi + 1:2 * gi + 2, :] = jnp.exp(b[two - 1:two])

    lane_k = lax.broadcasted_iota(jnp.int32, (c, LANES), 1)
    lane_v = lax.broadcasted_iota(jnp.int32, (c, 2 * DV_B), 1)
    att_row = lax.broadcasted_iota(jnp.int32, (c, 2 * c), 0)
    att_col = lax.broadcasted_iota(jnp.int32, (c, 2 * c), 1)
    causal = (att_col % c) <= att_row
    inc_row = lax.broadcasted_iota(jnp.int32, (2 * DV_B, LANES), 0)
    inc_col = lax.broadcasted_iota(jnp.int32, (2 * DV_B, LANES), 1)
    own_block = (inc_row // DV_B) == (inc_col // DK_B)
    zero = jnp.zeros((), BF16)
    for ci in range(n_chunks):
        rows = slice(ci * c, (ci + 1) * c)
        ki = ki_ref[rows, :]
        v = v_ref[rows, :]
        k_stack = jnp.concatenate([jnp.where(lane_k < DK_B, ki, zero), jnp.where(lane_k >= DK_B, ki, zero)], axis=0)
        att = jnp.where(causal, _dot_nt(qi_ref[rows, :], k_stack), 0.0)
        v_blocks = jnp.concatenate([jnp.where(lane_v < DV_B, v, zero), jnp.where(lane_v >= DV_B, v, zero)], axis=0)
        oi_ref[ci] = _dot(att.astype(BF16), v_blocks)
        v_t = v.astype(F32).T.astype(BF16)
        kv_ref[ci] = jnp.where(own_block, _dot(v_t, kd_ref[rows, :]), 0.0)

    state = state_ref[...]
    for ci in range(n_chunks):
        sb_ref[ci] = state.astype(BF16)
        state = state * dec_ref[ci:ci + 1, :] + kv_ref[ci]
    state_ref[...] = state

    for ci in range(n_chunks):
        rows = slice(ci * c, (ci + 1) * c)
        o = oi_ref[ci] + _dot_nt(qg_ref[rows, :], sb_ref[ci])
        rg = r_ref[rows, :]
        gate = rg * jax.nn.sigmoid(rg)
        for h in range(2):
            cols = slice(h * DV_B, (h + 1) * DV_B)
            o_ref[rows, cols] = (_rms(o[:, cols]) * gain * gate[:, cols]).astype(BF16)


def _gla(qb, kb, la, vb, rb, g_out, batch, seq, tg=512):
    t = batch * seq
    n_pairs = N_HEADS_B // 2
    per = seq // tg
    qk = pl.BlockSpec((tg, LANES), lambda b, p, i: (b * per + i, p))
    vr = pl.BlockSpec((tg, 2 * DV_B), lambda b, p, i: (b * per + i, p))
    n_chunks = tg // GLA_CHUNK
    return pl.pallas_call(
        functools.partial(_gla_kernel, n_chunks=n_chunks),
        grid=(batch, n_pairs, per),
        in_specs=[qk, qk, qk, vr, vr, _const_spec((1, DV_B))],
        out_specs=vr,
        out_shape=jax.ShapeDtypeStruct((t, B_V), BF16),
        scratch_shapes=[pltpu.VMEM((2 * DV_B, LANES), F32)]
        + [pltpu.VMEM((tg, LANES), BF16)] * 4
        + [pltpu.VMEM((n_chunks, LANES), F32), pltpu.VMEM((n_chunks, 2 * DV_B, LANES), F32),
           pltpu.VMEM((n_chunks, 2 * DV_B, LANES), BF16), pltpu.VMEM((n_chunks, GLA_CHUNK, 2 * DV_B), F32)],
        compiler_params=_params(("arbitrary", "arbitrary", "arbitrary")),
        name="gla",
    )(qb, kb, la, vb, rb, g_out)


def _sgu_kernel(h_ref, g_ref, w_in_ref, g_sgu_ref, ws_ref, bias_ref, o1_ref, o2_ref, hn_ref, v_ref, *, n_chunks):
    d = D_MODEL
    hn_ref[...] = (_rms(h_ref[...]) * g_ref[...]).astype(BF16)
    v = jax.nn.gelu(_dot(hn_ref[...], w_in_ref[:, d:]))
    v_ref[...] = (_rms(v) * g_sgu_ref[...]).astype(BF16)
    row = lax.broadcasted_iota(jnp.int32, (SGU_CHUNK, SGU_CHUNK), 0)
    col = lax.broadcasted_iota(jnp.int32, (SGU_CHUNK, SGU_CHUNK), 1)
    tril = col <= row
    half = d // 2
    for g in range(N_GROUPS_C):
        ws = jnp.where(tril, ws_ref[g], 0.0).astype(BF16)
        cols = slice(g * SGU_CHUNK, (g + 1) * SGU_CHUNK)
        u_all = jax.nn.gelu(_dot(hn_ref[...], w_in_ref[:, g * SGU_CHUNK:(g + 1) * SGU_CHUNK]))
        for ci in range(n_chunks):
            rows = slice(ci * SGU_CHUNK, (ci + 1) * SGU_CHUNK)
            sv = _dot(ws, v_ref[rows, cols]) + bias_ref[:, cols]
            y = (u_all[ci * SGU_CHUNK:(ci + 1) * SGU_CHUNK] * sv).astype(BF16)
            if g < N_GROUPS_C // 2:
                o1_ref[rows, cols] = y
            else:
                o2_ref[rows, g * SGU_CHUNK - half:(g + 1) * SGU_CHUNK - half] = y


def _sgu(h, g, w_in, g_sgu, ws, bias, tm=256):
    t = h.shape[0]
    row = lambda n: pl.BlockSpec((tm, n), lambda i: (i, 0))
    half = D_MODEL // 2
    return pl.pallas_call(
        functools.partial(_sgu_kernel, n_chunks=tm // SGU_CHUNK),
        grid=(t // tm,),
        in_specs=[row(D_MODEL), _const_spec((1, D_MODEL)), _const_spec(w_in.shape),
                  _const_spec((1, D_MODEL)), _const_spec(ws.shape), _const_spec(bias.shape)],
        out_specs=[row(half), row(half)],
        out_shape=[jax.ShapeDtypeStruct((t, half), BF16)] * 2,
        scratch_shapes=[pltpu.VMEM((tm, D_MODEL), BF16), pltpu.VMEM((tm, D_MODEL), BF16)],
        compiler_params=_params(("arbitrary",)),
        name="sgu",
    )(h, g, w_in, g_sgu, ws, bias)


def _shift_rows(z, prev, k):
    rolled = pltpu.roll(z, k, axis=0)
    head = jnp.where(lax.broadcasted_iota(jnp.int32, prev.shape, 0) < k,
                     pltpu.roll(prev, k, axis=0), rolled[:8])
    return jnp.concatenate([head, rolled[8:]], axis=0)


def _ffn_kernel(h_ref, o1_ref, o2_ref, w_o_ref, g_ref, w_up_ref, cw_ref, cb_ref, w_down_ref, gf_ref,
                out_ref, hn_ref, act_ref, carry_ref, *, final_norm):
    @pl.when(pl.program_id(1) == 0)
    def _():
        carry_ref[...] = jnp.zeros_like(carry_ref)

    tm = h_ref.shape[0]
    o = jnp.concatenate([o1_ref[...], o2_ref[...]], axis=1)
    h1 = h_ref[...] + _dot(o, w_o_ref[...])
    out_ref[...] = h1
    hn_ref[...] = (_rms(h1) * g_ref[...]).astype(BF16)

    def conv(idx, off):
        z = _dot(hn_ref[...], w_up_ref[:, off:off + FF_CHUNK])
        prev = carry_ref[idx]
        carry_ref[idx] = z[tm - 8:]
        w = cw_ref[:, off:off + FF_CHUNK]
        acc = _shift_rows(z, prev, 2) * w[0:1] + _shift_rows(z, prev, 1) * w[1:2] + z * w[2:3]
        return acc + cb_ref[:, off:off + FF_CHUNK]

    for c in range(N_FF_CHUNKS):
        gate = conv(c, c * FF_CHUNK)
        up = conv(N_FF_CHUNKS + c, D_FF + c * FF_CHUNK)
        act_ref[:, c * FF_CHUNK:(c + 1) * FF_CHUNK] = (gate * jax.nn.sigmoid(gate) * up).astype(BF16)
    h2 = out_ref[...] + _dot(act_ref[...], w_down_ref[...])
    if final_norm:
        h2 = _rms(h2) * gf_ref[...]
    out_ref[...] = h2


def _ffn(h, o1, o2, w_o, g, w_up, conv_w, conv_b, w_down, g_final, batch, seq, final_norm, tm=512):
    t = batch * seq
    per = seq // tm
    row = lambda n: pl.BlockSpec((tm, n), lambda b, i: (b * per + i, 0))
    return pl.pallas_call(
        functools.partial(_ffn_kernel, final_norm=final_norm),
        grid=(batch, per),
        in_specs=[row(D_MODEL), row(D_MODEL // 2), row(D_MODEL // 2), _const_spec(w_o.shape),
                  _const_spec((1, D_MODEL)), _const_spec(w_up.shape), _const_spec(conv_w.shape),
                  _const_spec(conv_b.shape), _const_spec(w_down.shape), _const_spec((1, D_MODEL))],
        out_specs=row(D_MODEL),
        out_shape=jax.ShapeDtypeStruct((t, D_MODEL), F32),
        scratch_shapes=[pltpu.VMEM((tm, D_MODEL), BF16), pltpu.VMEM((tm, D_FF), BF16),
                        pltpu.VMEM((2 * N_FF_CHUNKS, 8, FF_CHUNK), F32)],
        compiler_params=_params(("arbitrary", "arbitrary")),
        name="ffn",
    )(h, o1, o2, w_o, g, w_up, conv_w, conv_b, w_down, g_final)


def kernel(x, positions, norm_mix, norm_ffn, w_in_ab, w_gate_up, b_gate_up, g_out_b, w_out_ab, w_in_c,
           g_sgu, w_spatial, b_spatial, w_out_c, w_up, conv_w, conv_b, w_down, norm_final):
    batch, seq, d = x.shape
    depth = norm_mix.shape[0]
    t = batch * seq
    h = x.reshape(t, d)
    cos, sin = _rope_tables(positions)
    row = lambda v: v.reshape(1, -1)
    qkv_a = 3 * A_WIDTH
    b_end = qkv_a + 2 * B_QK + 2 * B_V
    for layer in range(depth):
        i = layer // 2
        if layer % 2 == 0:
            w_in = w_in_ab[i]
            wa = w_in[:, :qkv_a].astype(BF16)
            wb = w_in[:, qkv_a:b_end].astype(BF16)
            wg = jnp.pad(w_in[:, b_end:], ((0, 0), (0, LANES - GATE_RANK))).astype(BF16)
            wgu = jnp.pad(w_gate_up[i], ((0, LANES - GATE_RANK), (0, 0))).astype(BF16)
            qkv, (qb, kb, vb, rb, la) = _proj_ab(
                h, row(norm_mix[layer]), wa, wb, wg, wgu, row(b_gate_up[i]), cos, sin)
            o1 = _attention(qkv, batch, seq)
            o2 = _gla(qb, kb, la, vb, rb, row(g_out_b[i]), batch, seq)
            w_o = w_out_ab[i]
        else:
            bias = jnp.repeat(b_spatial[i].T, SGU_CHUNK, axis=1)
            o1, o2 = _sgu(h, row(norm_mix[layer]), w_in_c[i].astype(BF16), row(g_sgu[i]), w_spatial[i], bias)
            w_o = w_out_c[i]
        h = _ffn(h, o1, o2, w_o.astype(BF16), row(norm_ffn[layer]), w_up[layer].astype(BF16), conv_w[layer],
                 row(conv_b[layer]), w_down[layer].astype(BF16), row(norm_final), batch, seq,
                 final_norm=(layer == depth - 1))
    return h.reshape(batch, seq, d)
```

```python
import functools

import numpy as np
import jax
import jax.numpy as jnp
from jax import lax
from jax.experimental import pallas as pl
from jax.experimental.pallas import tpu as pltpu

F32 = jnp.float32
BF16 = jnp.bfloat16

D_MODEL = 1024
N_HEADS_A = 8
HEAD_DIM_A = 64
ROT_DIM = HEAD_DIM_A // 4
ROPE_THETA = 500000.0
DILATIONS = (1, 4, 16)
ATT_BLK = 128
ATT_SPAN = ATT_BLK * DILATIONS[-1]
N_HEADS_B = 4
DV_B = 128
DK_B = 64
GATE_RANK = 16
GATE_TAU = 16.0
GLA_CHUNK = 64
N_GROUPS_C = 8
SGU_CHUNK = 128
D_FF = 2816
CONV_W = 3
A_WIDTH = N_HEADS_A * HEAD_DIM_A
B_QK = N_HEADS_B * DK_B
B_V = N_HEADS_B * DV_B
NEG = -1e30
EPS = 1e-6
LANES = 128
N_PAIRS_A = A_WIDTH // LANES
FF_CHUNK = 256
N_FF_CHUNKS = D_FF // FF_CHUNK

VMEM_LIMIT = 56 * 1024 * 1024


def _dot(a, b):
    return jnp.dot(a, b, preferred_element_type=F32)


def _dot_nt(a, b):
    return lax.dot_general(a, b, (((1,), (1,)), ((), ())), preferred_element_type=F32)


def _rms(x):
    return x * lax.rsqrt(jnp.mean(x * x, axis=-1, keepdims=True) + EPS)


def _const_spec(shape):
    nd = len(shape)
    return pl.BlockSpec(shape, lambda *_: (0,) * nd, pipeline_mode=pl.Buffered(1))


def _params(sem, vmem=VMEM_LIMIT):
    return pltpu.CompilerParams(dimension_semantics=sem, vmem_limit_bytes=vmem)


def _rope_tab_kernel(pos_ref, inv_ref, sgn_ref, c_ref, s_ref):
    ang = pos_ref[...].astype(F32) * inv_ref[...]
    c_ref[...] = jnp.cos(ang)
    s_ref[...] = sgn_ref[...] * jnp.sin(ang)


def _rope_tables(positions):
    t = positions.size
    half = ROT_DIM // 2
    inv = np.float64(ROPE_THETA) ** (-np.arange(half, dtype=np.float64) * (2.0 / ROT_DIM))
    dim = np.arange(LANES) % HEAD_DIM_A
    inv_lane = np.where(dim < ROT_DIM, inv[dim % half], 0.0).astype(np.float32)[None, :]
    sgn_lane = np.where(dim < half, -1.0, np.where(dim < ROT_DIM, 1.0, 0.0)).astype(np.float32)[None, :]
    tm = 1024
    pos = positions.reshape(t, 1)
    return pl.pallas_call(
        _rope_tab_kernel,
        grid=(t // tm,),
        in_specs=[pl.BlockSpec((tm, 1), lambda i: (i, 0)),
                  _const_spec((1, LANES)), _const_spec((1, LANES))],
        out_specs=[pl.BlockSpec((tm, LANES), lambda i: (i, 0))] * 2,
        out_shape=[jax.ShapeDtypeStruct((t, LANES), F32)] * 2,
        compiler_params=_params(("arbitrary",)),
        name="rope_tables",
    )(pos, jnp.asarray(inv_lane), jnp.asarray(sgn_lane))


def _proj_ab_kernel(h_ref, g_ref, wa_ref, wb_ref, wg_ref, wgu_ref, bgu_ref, c_ref, s_ref, *refs):
    nd = len(DILATIONS)
    qkv_refs = refs[:3 * nd]
    qb_ref, kb_ref, vb_ref, rb_ref, la_ref, hn_ref, stage_ref = refs[3 * nd:]
    tm = h_ref.shape[0]
    hn_ref[...] = (_rms(h_ref[...]) * g_ref[...]).astype(BF16)
    cos = c_ref[...]
    sin = s_ref[...]
    lane = lax.broadcasted_iota(jnp.int32, cos.shape, 1)
    first_half = (lane % HEAD_DIM_A) < (ROT_DIM // 2)

    def rope(x):
        partner = jnp.where(first_half,
                            pltpu.roll(x, LANES - ROT_DIM // 2, axis=1),
                            pltpu.roll(x, ROT_DIM // 2, axis=1))
        return x * cos + partner * sin

    for which in range(3):
        for j in range(N_PAIRS_A):
            x = _dot(hn_ref[...], wa_ref[:, which * A_WIDTH + j * LANES:which * A_WIDTH + (j + 1) * LANES])
            if which < 2:
                x = rope(x)
            qkv_refs[which * nd][:, j * LANES:(j + 1) * LANES] = x.astype(BF16)
            stage = stage_ref.at[which * N_PAIRS_A + j]
            stage[...] = x
            for di, d in enumerate(DILATIONS[1:], start=1):
                for r in range(d):
                    piece = stage[pl.ds(r, tm // d, stride=d), :]
                    qkv_refs[which * nd + di][r, :, j * LANES:(j + 1) * LANES] = piece.astype(BF16)
    qb_ref[...] = _dot(hn_ref[...], wb_ref[:, :B_QK])
    kb_ref[...] = _dot(hn_ref[...], wb_ref[:, B_QK:2 * B_QK])
    vb_ref[...] = _dot(hn_ref[...], wb_ref[:, 2 * B_QK:2 * B_QK + B_V]).astype(BF16)
    rb_ref[...] = _dot(hn_ref[...], wb_ref[:, 2 * B_QK + B_V:])
    gl = _dot(hn_ref[...], wg_ref[...])
    g = _dot(gl.astype(BF16), wgu_ref[...]) + bgu_ref[...]
    la_ref[...] = (jnp.minimum(g, 0.0) - jnp.log1p(jnp.exp(-jnp.abs(g)))) * (1.0 / GATE_TAU)


def _proj_ab(h, g, wa, wb, wg, wgu, bgu, cos, sin, tm=512):
    t = h.shape[0]
    row = lambda n: pl.BlockSpec((tm, n), lambda i: (i, 0))
    specs = [row(A_WIDTH)] + [pl.BlockSpec((d, tm // d, A_WIDTH), lambda i: (0, i, 0)) for d in DILATIONS[1:]]
    shapes = [jax.ShapeDtypeStruct((t, A_WIDTH), BF16)]
    shapes += [jax.ShapeDtypeStruct((d, t // d, A_WIDTH), BF16) for d in DILATIONS[1:]]
    specs, shapes = specs * 3, shapes * 3
    out_widths = (B_QK, B_QK, B_V, B_V, B_QK)
    out_dtypes = (F32, F32, BF16, F32, F32)
    outs = pl.pallas_call(
        _proj_ab_kernel,
        grid=(t // tm,),
        in_specs=[row(D_MODEL), _const_spec((1, D_MODEL)), _const_spec(wa.shape), _const_spec(wb.shape),
                  _const_spec(wg.shape), _const_spec(wgu.shape), _const_spec(bgu.shape),
                  row(LANES), row(LANES)],
        out_specs=specs + [row(n) for n in out_widths],
        out_shape=shapes + [jax.ShapeDtypeStruct((t, n), d) for n, d in zip(out_widths, out_dtypes)],
        scratch_shapes=[pltpu.VMEM((tm, D_MODEL), BF16), pltpu.VMEM((3 * N_PAIRS_A, tm, LANES), F32)],
        compiler_params=_params(("arbitrary",)),
        name="proj_ab",
    )(h, g, wa, wb, wg, wgu, bgu, cos, sin)
    return outs[:3 * len(DILATIONS)], outs[3 * len(DILATIONS):]


def _attn_bias():
    a = np.arange(ATT_BLK)[None, :]
    c = np.arange(2 * ATT_BLK)[:, None]
    band = (c >= a) & (c <= a + ATT_BLK)
    first = band & (c >= ATT_BLK)
    return np.where(np.stack([band, first]), 0.0, NEG).astype(np.float32)


def _attn_kernel(*refs):
    nb = len(DILATIONS)
    in_refs = refs[:3 * nb]
    bias_ref = refs[3 * nb]
    out_ref = refs[3 * nb + 1]
    scratch = refs[3 * nb + 2:]
    acc_refs, lse_refs = scratch[:nb], scratch[nb:2 * nb]
    s_ref, p_ref, inv_ref, stat_ref = scratch[2 * nb:2 * nb + 4]
    kring_refs, vring_refs = scratch[2 * nb + 4:3 * nb + 4], scratch[3 * nb + 4:]
    n = pl.program_id(1)
    r = pl.program_id(2)
    n_res = DILATIONS[-1]
    lane = lax.broadcasted_iota(jnp.int32, (ATT_BLK, LANES), 1)
    low_head = lane < HEAD_DIM_A
    zero = jnp.zeros((), BF16)
    scale = jnp.asarray(HEAD_DIM_A ** -0.5, BF16)
    eye = jnp.where(lax.broadcasted_iota(jnp.int32, (ATT_BLK, ATT_BLK), 0)
                    == lax.broadcasted_iota(jnp.int32, (ATT_BLK, ATT_BLK), 1), 1.0, 0.0).astype(BF16)
    slots = [r % d for d in DILATIONS]
    blks = [n * (n_res // d) + r // d for d in DILATIONS]

    @pl.when(n == 0)
    def _():
        for g in range(nb):
            @pl.when(blks[g] == 0)
            def _():
                kring_refs[g][slots[g]] = jnp.zeros((ATT_BLK, A_WIDTH), BF16)
                vring_refs[g][slots[g]] = jnp.zeros((ATT_BLK, A_WIDTH), BF16)

    for g, d in enumerate(DILATIONS):
        q_ref, k_ref = in_refs[3 * g:3 * g + 2]
        bias_t = bias_ref[jnp.where(blks[g] == 0, 1, 0)]
        for hp in range(N_PAIRS_A):
            sl = slice(hp * LANES, (hp + 1) * LANES)
            q2 = q_ref[0, 0, :, sl] * scale
            kcat = jnp.concatenate([kring_refs[g][slots[g], :, sl], k_ref[0, 0, :, sl]], axis=0)
            rhs = jnp.concatenate([kcat, bias_t], axis=1)
            for hh in range(2):
                qm = jnp.where(low_head == (hh == 0), q2, zero)
                s_ref[(g * N_PAIRS_A + hp) * 2 + hh] = _dot_nt(jnp.concatenate([qm, eye], axis=1), rhs)

    for u in range(nb * N_PAIRS_A * 2):
        s = s_ref[u]
        m = jnp.max(s, axis=-1, keepdims=True)
        p = jnp.exp(s - m)
        l = jnp.sum(p, axis=-1, keepdims=True)
        p_ref[u] = p.astype(BF16)
        inv_ref[u] = jnp.broadcast_to(1.0 / l, (ATT_BLK, LANES))
        stat_ref[u] = jnp.broadcast_to(m + jnp.log(l), (ATT_BLK, LANES))

    for g, d in enumerate(DILATIONS):
        v_ref = in_refs[3 * g + 2]
        if d == 1:
            rows = pl.ds(pl.multiple_of(r * ATT_BLK, ATT_BLK), ATT_BLK)
        else:
            rows = pl.ds((r // d) * (ATT_BLK * d) + r % d, ATT_BLK, stride=d)
        for hp in range(N_PAIRS_A):
            sl = slice(hp * LANES, (hp + 1) * LANES)
            vcat = jnp.concatenate([vring_refs[g][slots[g], :, sl], v_ref[0, 0, :, sl]], axis=0)
            u0 = (g * N_PAIRS_A + hp) * 2
            o0 = _dot(p_ref[u0], vcat) * inv_ref[u0]
            o1 = _dot(p_ref[u0 + 1], vcat) * inv_ref[u0 + 1]
            acc_refs[g][hp, rows, :] = jnp.where(low_head, o0, o1)
            lse_refs[g][hp, rows, :] = jnp.where(low_head, stat_ref[u0], stat_ref[u0 + 1])
        kring_refs[g][slots[g]] = in_refs[3 * g + 1][0, 0]
        vring_refs[g][slots[g]] = v_ref[0, 0]

    @pl.when(r == n_res - 1)
    def _():
        chunk = 256
        for hp in range(A_WIDTH // LANES):
            def body(i, carry):
                rows = pl.ds(pl.multiple_of(i * chunk, chunk), chunk)
                lse = [ref[hp, rows, :] for ref in lse_refs]
                top = functools.reduce(jnp.maximum, lse)
                w = [jnp.exp(x - top) for x in lse]
                num = functools.reduce(jnp.add, [wi * ref[hp, rows, :] for wi, ref in zip(w, acc_refs)])
                out_ref[0, rows, hp * LANES:(hp + 1) * LANES] = (num / functools.reduce(jnp.add, w)).astype(BF16)
                return carry
            lax.fori_loop(0, ATT_SPAN // chunk, body, 0)


def _attention(qkv, batch, seq):
    n_res = DILATIONS[-1]
    n_span = seq // ATT_SPAN
    nd = len(DILATIONS)
    operands, in_specs = [], []
    for di, d in enumerate(DILATIONS):
        per = n_res // d

        def cur(b, n, r, d=d, per=per):
            return (r % d, b, n * per + r // d, 0)

        operands += [qkv[which * nd + di].reshape(d, batch, seq // d, A_WIDTH) for which in range(3)]
        in_specs += [pl.BlockSpec((1, 1, ATT_BLK, A_WIDTH), cur)] * 3
    bias = jnp.asarray(_attn_bias(), BF16)
    n_units = len(DILATIONS) * N_HEADS_A
    rings = [pltpu.VMEM((d, ATT_BLK, A_WIDTH), BF16) for d in DILATIONS]
    out = pl.pallas_call(
        _attn_kernel,
        grid=(batch, n_span, n_res),
        in_specs=in_specs + [_const_spec(bias.shape)],
        out_specs=pl.BlockSpec((1, ATT_SPAN, A_WIDTH), lambda b, n, r: (b, n, 0)),
        out_shape=jax.ShapeDtypeStruct((batch, seq, A_WIDTH), BF16),
        scratch_shapes=[pltpu.VMEM((N_PAIRS_A, ATT_SPAN, LANES), F32)] * (2 * len(DILATIONS))
        + [pltpu.VMEM((n_units, ATT_BLK, 2 * ATT_BLK), F32), pltpu.VMEM((n_units, ATT_BLK, 2 * ATT_BLK), BF16),
           pltpu.VMEM((n_units, ATT_BLK, LANES), F32), pltpu.VMEM((n_units, ATT_BLK, LANES), F32)]
        + rings + rings,
        compiler_params=_params(("arbitrary", "arbitrary", "arbitrary")),
        name="dilated_attention",
    )(*operands, bias)
    return out.reshape(batch * seq, A_WIDTH)


def _gla_kernel(q_ref, k_ref, la_ref, v_ref, r_ref, g_ref, o_ref, state_ref,
                qi_ref, ki_ref, kd_ref, qg_ref, dec_ref, kv_ref, sb_ref, oi_ref, *, n_chunks):
    @pl.when(pl.program_id(2) == 0)
    def _():
        state_ref[...] = jnp.zeros_like(state_ref)

    c = GLA_CHUNK
    two = 2 * c
    row2 = lax.broadcasted_iota(jnp.int32, (two, two), 0)
    col2 = lax.broadcasted_iota(jnp.int32, (two, two), 1)
    tril2 = jnp.where((col2 <= row2) & (col2 // c == row2 // c), 1.0, 0.0).astype(BF16)
    gain = g_ref[...]

    for gi in range(n_chunks // 2):
        rows = slice(gi * two, (gi + 1) * two)
        la = la_ref[rows, :]
        hi = la.astype(BF16)
        rem = la - hi.astype(F32)
        mid = rem.astype(BF16)
        lo = (rem - mid.astype(F32)).astype(BF16)
        b3 = _dot(tril2, jnp.concatenate([hi, mid, lo], axis=1))
        b = b3[:, :LANES] + b3[:, LANES:2 * LANES] + b3[:, 2 * LANES:]
        b_last = jnp.concatenate([jnp.broadcast_to(b[c - 1:c], (c, LANES)),
                                  jnp.broadcast_to(b[two - 1:two], (c, LANES))], axis=0)
        b_mid = jnp.concatenate([jnp.broadcast_to(b[c // 2 - 1:c // 2], (c, LANES)),
                                 jnp.broadcast_to(b[c + c // 2 - 1:c + c // 2], (c, LANES))], axis=0)
        q = q_ref[rows, :] * (DK_B ** -0.5)
        k = k_ref[rows, :]
        qi_ref[rows, :] = (q * jnp.exp(b - b_mid)).astype(BF16)
        ki_ref[rows, :] = (k * jnp.exp(b_mid - b)).astype(BF16)
        kd_ref[rows, :] = (k * jnp.exp(b_last - b)).astype(BF16)
        qg_ref[rows, :] = (q * jnp.exp(b)).astype(BF16)
        dec_ref[2 * gi:2 * gi + 1, :] = jnp.exp(b[c - 1:c])
        dec_ref[2 * gi + 1:2 * gi + 2, :] = jnp.exp(b[two - 1:two])

    lane_k = lax.broadcasted_iota(jnp.int32, (c, LANES), 1)
    lane_v = lax.broadcasted_iota(jnp.int32, (c, 2 * DV_B), 1)
    att_row = lax.broadcasted_iota(jnp.int32, (c, 2 * c), 0)
    att_col = lax.broadcasted_iota(jnp.int32, (c, 2 * c), 1)
    causal = (att_col % c) <= att_row
    inc_row = lax.broadcasted_iota(jnp.int32, (2 * DV_B, LANES), 0)
    inc_col = lax.broadcasted_iota(jnp.int32, (2 * DV_B, LANES), 1)
    own_block = (inc_row // DV_B) == (inc_col // DK_B)
    zero = jnp.zeros((), BF16)
    for ci in range(n_chunks):
        rows = slice(ci * c, (ci + 1) * c)
        ki = ki_ref[rows, :]
        v = v_ref[rows, :]
        k_stack = jnp.concatenate([jnp.where(lane_k < DK_B, ki, zero), jnp.where(lane_k >= DK_B, ki, zero)], axis=0)
        att = jnp.where(causal, _dot_nt(qi_ref[rows, :], k_stack), 0.0)
        v_blocks = jnp.concatenate([jnp.where(lane_v < DV_B, v, zero), jnp.where(lane_v >= DV_B, v, zero)], axis=0)
        oi_ref[ci] = _dot(att.astype(BF16), v_blocks)
        v_t = v.astype(F32).T.astype(BF16)
        kv_ref[ci] = jnp.where(own_block, _dot(v_t, kd_ref[rows, :]), 0.0)

    state = state_ref[...]
    for ci in range(n_chunks):
        sb_ref[ci] = state.astype(BF16)
        state = state * dec_ref[ci:ci + 1, :] + kv_ref[ci]
    state_ref[...] = state

    for ci in range(n_chunks):
        rows = slice(ci * c, (ci + 1) * c)
        o = oi_ref[ci] + _dot_nt(qg_ref[rows, :], sb_ref[ci])
        rg = r_ref[rows, :]
        gate = rg * jax.nn.sigmoid(rg)
        for h in range(2):
            cols = slice(h * DV_B, (h + 1) * DV_B)
            o_ref[rows, cols] = (_rms(o[:, cols]) * gain * gate[:, cols]).astype(BF16)


def _gla(qb, kb, la, vb, rb, g_out, batch, seq, tg=512):
    t = batch * seq
    n_pairs = N_HEADS_B // 2
    per = seq // tg
    qk = pl.BlockSpec((tg, LANES), lambda b, p, i: (b * per + i, p))
    vr = pl.BlockSpec((tg, 2 * DV_B), lambda b, p, i: (b * per + i, p))
    n_chunks = tg // GLA_CHUNK
    return pl.pallas_call(
        functools.partial(_gla_kernel, n_chunks=n_chunks),
        grid=(batch, n_pairs, per),
        in_specs=[qk, qk, qk, vr, vr, _const_spec((1, DV_B))],
        out_specs=vr,
        out_shape=jax.ShapeDtypeStruct((t, B_V), BF16),
        scratch_shapes=[pltpu.VMEM((2 * DV_B, LANES), F32)]
        + [pltpu.VMEM((tg, LANES), BF16)] * 4
        + [pltpu.VMEM((n_chunks, LANES), F32), pltpu.VMEM((n_chunks, 2 * DV_B, LANES), F32),
           pltpu.VMEM((n_chunks, 2 * DV_B, LANES), BF16), pltpu.VMEM((n_chunks, GLA_CHUNK, 2 * DV_B), F32)],
        compiler_params=_params(("arbitrary", "arbitrary", "arbitrary")),
        name="gla",
    )(qb, kb, la, vb, rb, g_out)


def _sgu_kernel(h_ref, g_ref, w_in_ref, g_sgu_ref, ws_ref, bias_ref, o1_ref, o2_ref, hn_ref, v_ref, *, n_chunks):
    d = D_MODEL
    hn_ref[...] = (_rms(h_ref[...]) * g_ref[...]).astype(BF16)
    v = jax.nn.gelu(_dot(hn_ref[...], w_in_ref[:, d:]))
    v_ref[...] = (_rms(v) * g_sgu_ref[...]).astype(BF16)
    row = lax.broadcasted_iota(jnp.int32, (SGU_CHUNK, SGU_CHUNK), 0)
    col = lax.broadcasted_iota(jnp.int32, (SGU_CHUNK, SGU_CHUNK), 1)
    tril = col <= row
    half = d // 2
    for g in range(N_GROUPS_C):
        ws = jnp.where(tril, ws_ref[g], 0.0).astype(BF16)
        cols = slice(g * SGU_CHUNK, (g + 1) * SGU_CHUNK)
        u_all = jax.nn.gelu(_dot(hn_ref[...], w_in_ref[:, g * SGU_CHUNK:(g + 1) * SGU_CHUNK]))
        for ci in range(n_chunks):
            rows = slice(ci * SGU_CHUNK, (ci + 1) * SGU_CHUNK)
            sv = _dot(ws, v_ref[rows, cols]) + bias_ref[:, cols]
            y = (u_all[ci * SGU_CHUNK:(ci + 1) * SGU_CHUNK] * sv).astype(BF16)
            if g < N_GROUPS_C // 2:
                o1_ref[rows, cols] = y
            else:
                o2_ref[rows, g * SGU_CHUNK - half:(g + 1) * SGU_CHUNK - half] = y


def _sgu(h, g, w_in, g_sgu, ws, bias, tm=512):
    t = h.shape[0]
    row = lambda n: pl.BlockSpec((tm, n), lambda i: (i, 0))
    half = D_MODEL // 2
    return pl.pallas_call(
        functools.partial(_sgu_kernel, n_chunks=tm // SGU_CHUNK),
        grid=(t // tm,),
        in_specs=[row(D_MODEL), _const_spec((1, D_MODEL)), _const_spec(w_in.shape),
                  _const_spec((1, D_MODEL)), _const_spec(ws.shape), _const_spec(bias.shape)],
        out_specs=[row(half), row(half)],
        out_shape=[jax.ShapeDtypeStruct((t, half), BF16)] * 2,
        scratch_shapes=[pltpu.VMEM((tm, D_MODEL), BF16), pltpu.VMEM((tm, D_MODEL), BF16)],
        compiler_params=_params(("arbitrary",)),
        name="sgu",
    )(h, g, w_in, g_sgu, ws, bias)


def _shift_rows(z, prev, k):
    rolled = pltpu.roll(z, k, axis=0)
    head = jnp.where(lax.broadcasted_iota(jnp.int32, prev.shape, 0) < k,
                     pltpu.roll(prev, k, axis=0), rolled[:8])
    return jnp.concatenate([head, rolled[8:]], axis=0)


def _ffn_kernel(h_ref, o1_ref, o2_ref, w_o_ref, g_ref, w_up_ref, cw_ref, cb_ref, w_down_ref, gf_ref,
                out_ref, hn_ref, act_ref, carry_ref, *, final_norm):
    @pl.when(pl.program_id(1) == 0)
    def _():
        carry_ref[...] = jnp.zeros_like(carry_ref)

    tm = h_ref.shape[0]
    o = jnp.concatenate([o1_ref[...], o2_ref[...]], axis=1)
    h1 = h_ref[...] + _dot(o, w_o_ref[...])
    out_ref[...] = h1
    hn_ref[...] = (_rms(h1) * g_ref[...]).astype(BF16)

    def conv(idx, off):
        z = _dot(hn_ref[...], w_up_ref[:, off:off + FF_CHUNK])
        prev = carry_ref[idx]
        carry_ref[idx] = z[tm - 8:]
        w = cw_ref[:, off:off + FF_CHUNK]
        acc = _shift_rows(z, prev, 2) * w[0:1] + _shift_rows(z, prev, 1) * w[1:2] + z * w[2:3]
        return acc + cb_ref[:, off:off + FF_CHUNK]

    for c in range(N_FF_CHUNKS):
        gate = conv(c, c * FF_CHUNK)
        up = conv(N_FF_CHUNKS + c, D_FF + c * FF_CHUNK)
        act_ref[:, c * FF_CHUNK:(c + 1) * FF_CHUNK] = (gate * jax.nn.sigmoid(gate) * up).astype(BF16)
    h2 = out_ref[...] + _dot(act_ref[...], w_down_ref[...])
    if final_norm:
        h2 = _rms(h2) * gf_ref[...]
    out_ref[...] = h2


def _ffn(h, o1, o2, w_o, g, w_up, conv_w, conv_b, w_down, g_final, batch, seq, final_norm, tm=1024):
    t = batch * seq
    per = seq // tm
    row = lambda n: pl.BlockSpec((tm, n), lambda b, i: (b * per + i, 0))
    return pl.pallas_call(
        functools.partial(_ffn_kernel, final_norm=final_norm),
        grid=(batch, per),
        in_specs=[row(D_MODEL), row(D_MODEL // 2), row(D_MODEL // 2), _const_spec(w_o.shape),
                  _const_spec((1, D_MODEL)), _const_spec(w_up.shape), _const_spec(conv_w.shape),
                  _const_spec(conv_b.shape), _const_spec(w_down.shape), _const_spec((1, D_MODEL))],
        out_specs=row(D_MODEL),
        out_shape=jax.ShapeDtypeStruct((t, D_MODEL), F32),
        scratch_shapes=[pltpu.VMEM((tm, D_MODEL), BF16), pltpu.VMEM((tm, D_FF), BF16),
                        pltpu.VMEM((2 * N_FF_CHUNKS, 8, FF_CHUNK), F32)],
        compiler_params=_params(("arbitrary", "arbitrary")),
        name="ffn",
    )(h, o1, o2, w_o, g, w_up, conv_w, conv_b, w_down, g_final)


def kernel(x, positions, norm_mix, norm_ffn, w_in_ab, w_gate_up, b_gate_up, g_out_b, w_out_ab, w_in_c,
           g_sgu, w_spatial, b_spatial, w_out_c, w_up, conv_w, conv_b, w_down, norm_final):
    batch, seq, d = x.shape
    depth = norm_mix.shape[0]
    t = batch * seq
    h = x.reshape(t, d)
    cos, sin = _rope_tables(positions)
    row = lambda v: v.reshape(1, -1)
    qkv_a = 3 * A_WIDTH
    b_end = qkv_a + 2 * B_QK + 2 * B_V
    for layer in range(depth):
        i = layer // 2
        if layer % 2 == 0:
            w_in = w_in_ab[i]
            wa = w_in[:, :qkv_a].astype(BF16)
            wb = w_in[:, qkv_a:b_end].astype(BF16)
            wg = jnp.pad(w_in[:, b_end:], ((0, 0), (0, LANES - GATE_RANK))).astype(BF16)
            wgu = jnp.pad(w_gate_up[i], ((0, LANES - GATE_RANK), (0, 0))).astype(BF16)
            qkv, (qb, kb, vb, rb, la) = _proj_ab(
                h, row(norm_mix[layer]), wa, wb, wg, wgu, row(b_gate_up[i]), cos, sin)
            o1 = _attention(qkv, batch, seq)
            o2 = _gla(qb, kb, la, vb, rb, row(g_out_b[i]), batch, seq)
            w_o = w_out_ab[i]
        else:
            bias = jnp.repeat(b_spatial[i].T, SGU_CHUNK, axis=1)
            o1, o2 = _sgu(h, row(norm_mix[layer]), w_in_c[i].astype(BF16), row(g_sgu[i]), w_spatial[i], bias)
            w_o = w_out_c[i]
        h = _ffn(h, o1, o2, w_o.astype(BF16), row(norm_ffn[layer]), w_up[layer].astype(BF16), conv_w[layer],
                 row(conv_b[layer]), w_down[layer].astype(BF16), row(norm_final), batch, seq,
                 final_norm=(layer == depth - 1))
    return h.reshape(batch, seq, d)
```

```python
import functools

import numpy as np
import jax
import jax.numpy as jnp
from jax import lax
from jax.experimental import pallas as pl
from jax.experimental.pallas import tpu as pltpu

F32 = jnp.float32
BF16 = jnp.bfloat16

D_MODEL = 1024
N_HEADS_A = 8
HEAD_DIM_A = 64
ROT_DIM = HEAD_DIM_A // 4
ROPE_THETA = 500000.0
DILATIONS = (1, 4, 16)
ATT_BLK = 128
ATT_SPAN = ATT_BLK * DILATIONS[-1]
N_HEADS_B = 4
DV_B = 128
DK_B = 64
GATE_RANK = 16
GATE_TAU = 16.0
GLA_CHUNK = 64
N_GROUPS_C = 8
SGU_CHUNK = 128
D_FF = 2816
CONV_W = 3
A_WIDTH = N_HEADS_A * HEAD_DIM_A
B_QK = N_HEADS_B * DK_B
B_V = N_HEADS_B * DV_B
NEG = -1e30
EPS = 1e-6
LANES = 128
N_PAIRS_A = A_WIDTH // LANES
MXU_N = 256
FF_CHUNK = MXU_N
N_FF_CHUNKS = D_FF // FF_CHUNK

VMEM_LIMIT = 56 * 1024 * 1024


def _dot(a, b):
    return jnp.dot(a, b, preferred_element_type=F32)


def _dot_nt(a, b):
    return lax.dot_general(a, b, (((1,), (1,)), ((), ())), preferred_element_type=F32)


def _rms(x):
    return x * lax.rsqrt(jnp.mean(x * x, axis=-1, keepdims=True) + EPS)


def _const_spec(shape):
    nd = len(shape)
    return pl.BlockSpec(shape, lambda *_: (0,) * nd, pipeline_mode=pl.Buffered(1))


def _layer_spec(stacked, layer):
    nd = stacked.ndim - 1
    return pl.BlockSpec((None,) + stacked.shape[1:], lambda *_: (layer,) + (0,) * nd, pipeline_mode=pl.Buffered(1))


def _params(sem, vmem=VMEM_LIMIT):
    return pltpu.CompilerParams(dimension_semantics=sem, vmem_limit_bytes=vmem)


def _rope_tab_kernel(pos_ref, inv_ref, sgn_ref, c_ref, s_ref):
    ang = pos_ref[...].astype(F32) * inv_ref[...]
    c_ref[...] = jnp.cos(ang)
    s_ref[...] = sgn_ref[...] * jnp.sin(ang)


def _rope_tables(positions):
    t = positions.size
    half = ROT_DIM // 2
    inv = np.float64(ROPE_THETA) ** (-np.arange(half, dtype=np.float64) * (2.0 / ROT_DIM))
    dim = np.arange(LANES) % HEAD_DIM_A
    inv_lane = np.where(dim < ROT_DIM, inv[dim % half], 0.0).astype(np.float32)[None, :]
    sgn_lane = np.where(dim < half, -1.0, np.where(dim < ROT_DIM, 1.0, 0.0)).astype(np.float32)[None, :]
    tm = 1024
    pos = positions.reshape(t, 1)
    return pl.pallas_call(
        _rope_tab_kernel,
        grid=(t // tm,),
        in_specs=[pl.BlockSpec((tm, 1), lambda i: (i, 0)),
                  _const_spec((1, LANES)), _const_spec((1, LANES))],
        out_specs=[pl.BlockSpec((tm, LANES), lambda i: (i, 0))] * 2,
        out_shape=[jax.ShapeDtypeStruct((t, LANES), F32)] * 2,
        compiler_params=_params(("arbitrary",)),
        name="rope_tables",
    )(pos, jnp.asarray(inv_lane), jnp.asarray(sgn_lane))


def _proj_ab_kernel(h_ref, g_ref, w_ref, wg_ref, wgu_ref, bgu_ref, c_ref, s_ref, *refs):
    nd = len(DILATIONS)
    qkv_refs = refs[:3 * nd]
    qb_ref, kb_ref, vb_ref, rb_ref, la_ref, hn_ref, stage_ref = refs[3 * nd:]
    tm = h_ref.shape[0]
    hn_ref[...] = (_rms(h_ref[...]) * g_ref[...]).astype(BF16)
    cos = c_ref[...]
    sin = s_ref[...]
    lane = lax.broadcasted_iota(jnp.int32, cos.shape, 1)
    first_half = (lane % HEAD_DIM_A) < (ROT_DIM // 2)

    def rope(x):
        partner = jnp.where(first_half,
                            pltpu.roll(x, LANES - ROT_DIM // 2, axis=1),
                            pltpu.roll(x, ROT_DIM // 2, axis=1))
        return x * cos + partner * sin

    for which in range(3):
        for jj in range(A_WIDTH // MXU_N):
            col = which * A_WIDTH + jj * MXU_N
            x2 = _dot(hn_ref[...], w_ref[:, col:col + MXU_N])
            for half in range(MXU_N // LANES):
                j = jj * (MXU_N // LANES) + half
                x = x2[:, half * LANES:(half + 1) * LANES]
                if which < 2:
                    x = rope(x)
                qkv_refs[which * nd][:, j * LANES:(j + 1) * LANES] = x.astype(BF16)
                stage = stage_ref.at[which * N_PAIRS_A + j]
                stage[...] = x
                for di, d in enumerate(DILATIONS[1:], start=1):
                    for r in range(d):
                        piece = stage[pl.ds(r, tm // d, stride=d), :]
                        qkv_refs[which * nd + di][r, :, j * LANES:(j + 1) * LANES] = piece.astype(BF16)
    b0 = 3 * A_WIDTH
    qb_ref[...] = _dot(hn_ref[...], w_ref[:, b0:b0 + B_QK])
    kb_ref[...] = _dot(hn_ref[...], w_ref[:, b0 + B_QK:b0 + 2 * B_QK])
    vb_ref[...] = _dot(hn_ref[...], w_ref[:, b0 + 2 * B_QK:b0 + 2 * B_QK + B_V]).astype(BF16)
    rb_ref[...] = _dot(hn_ref[...], w_ref[:, b0 + 2 * B_QK + B_V:b0 + 2 * B_QK + 2 * B_V])
    gl = _dot(hn_ref[...], wg_ref[...])
    g = _dot(gl.astype(BF16), wgu_ref[...]) + bgu_ref[...]
    la_ref[...] = (jnp.minimum(g, 0.0) - jnp.log1p(jnp.exp(-jnp.abs(g)))) * (1.0 / GATE_TAU)


def _proj_ab(h, g, w_in, layer, wg, wgu, bgu, cos, sin, tm=512):
    t = h.shape[0]
    row = lambda n: pl.BlockSpec((tm, n), lambda i: (i, 0))
    specs = [row(A_WIDTH)] + [pl.BlockSpec((d, tm // d, A_WIDTH), lambda i: (0, i, 0)) for d in DILATIONS[1:]]
    shapes = [jax.ShapeDtypeStruct((t, A_WIDTH), BF16)]
    shapes += [jax.ShapeDtypeStruct((d, t // d, A_WIDTH), BF16) for d in DILATIONS[1:]]
    specs, shapes = specs * 3, shapes * 3
    out_widths = (B_QK, B_QK, B_V, B_V, B_QK)
    out_dtypes = (F32, F32, BF16, F32, F32)
    outs = pl.pallas_call(
        _proj_ab_kernel,
        grid=(t // tm,),
        in_specs=[row(D_MODEL), _const_spec((1, D_MODEL)), _layer_spec(w_in, layer),
                  _const_spec(wg.shape), _const_spec(wgu.shape), _const_spec(bgu.shape),
                  row(LANES), row(LANES)],
        out_specs=specs + [row(n) for n in out_widths],
        out_shape=shapes + [jax.ShapeDtypeStruct((t, n), d) for n, d in zip(out_widths, out_dtypes)],
        scratch_shapes=[pltpu.VMEM((tm, D_MODEL), BF16), pltpu.VMEM((3 * N_PAIRS_A, tm, LANES), F32)],
        compiler_params=_params(("arbitrary",)),
        name="proj_ab",
    )(h, g, w_in, wg, wgu, bgu, cos, sin)
    return outs[:3 * len(DILATIONS)], outs[3 * len(DILATIONS):]


def _attn_bias():
    a = np.arange(ATT_BLK)[None, :]
    c = np.arange(2 * ATT_BLK)[:, None]
    band = (c >= a) & (c <= a + ATT_BLK)
    first = band & (c >= ATT_BLK)
    return np.where(np.stack([band, first]), 0.0, NEG).astype(np.float32)


def _attn_kernel(*refs):
    nb = len(DILATIONS)
    in_refs = refs[:3 * nb]
    bias_ref = refs[3 * nb]
    out_ref = refs[3 * nb + 1]
    scratch = refs[3 * nb + 2:]
    acc_refs, lse_refs = scratch[:nb], scratch[nb:2 * nb]
    s_ref, p_ref, inv_ref, stat_ref = scratch[2 * nb:2 * nb + 4]
    kring_refs, vring_refs = scratch[2 * nb + 4:3 * nb + 4], scratch[3 * nb + 4:]
    n = pl.program_id(1)
    r = pl.program_id(2)
    n_res = DILATIONS[-1]
    lane = lax.broadcasted_iota(jnp.int32, (ATT_BLK, LANES), 1)
    low_head = lane < HEAD_DIM_A
    zero = jnp.zeros((), BF16)
    scale = jnp.asarray(HEAD_DIM_A ** -0.5, BF16)
    eye = jnp.where(lax.broadcasted_iota(jnp.int32, (ATT_BLK, ATT_BLK), 0)
                    == lax.broadcasted_iota(jnp.int32, (ATT_BLK, ATT_BLK), 1), 1.0, 0.0).astype(BF16)
    slots = [r % d for d in DILATIONS]
    blks = [n * (n_res // d) + r // d for d in DILATIONS]

    @pl.when(n == 0)
    def _():
        for g in range(nb):
            @pl.when(blks[g] == 0)
            def _():
                kring_refs[g][slots[g]] = jnp.zeros((ATT_BLK, A_WIDTH), BF16)
                vring_refs[g][slots[g]] = jnp.zeros((ATT_BLK, A_WIDTH), BF16)

    for g, d in enumerate(DILATIONS):
        q_ref, k_ref = in_refs[3 * g:3 * g + 2]
        bias_t = bias_ref[jnp.where(blks[g] == 0, 1, 0)]
        for hp in range(N_PAIRS_A):
            sl = slice(hp * LANES, (hp + 1) * LANES)
            q2 = q_ref[0, 0, :, sl] * scale
            kcat = jnp.concatenate([kring_refs[g][slots[g], :, sl], k_ref[0, 0, :, sl]], axis=0)
            rhs = jnp.concatenate([kcat, bias_t], axis=1)
            for hh in range(2):
                qm = jnp.where(low_head == (hh == 0), q2, zero)
                s_ref[(g * N_PAIRS_A + hp) * 2 + hh] = _dot_nt(jnp.concatenate([qm, eye], axis=1), rhs)

    for u in range(nb * N_PAIRS_A * 2):
        s = s_ref[u]
        m = jnp.max(s, axis=-1, keepdims=True)
        p = jnp.exp(s - m)
        l = jnp.sum(p, axis=-1, keepdims=True)
        p_ref[u] = p.astype(BF16)
        inv_ref[u] = jnp.broadcast_to(1.0 / l, (ATT_BLK, LANES))
        stat_ref[u] = jnp.broadcast_to(m + jnp.log(l), (ATT_BLK, LANES))

    for g, d in enumerate(DILATIONS):
        v_ref = in_refs[3 * g + 2]
        if d == 1:
            rows = pl.ds(pl.multiple_of(r * ATT_BLK, ATT_BLK), ATT_BLK)
        else:
            rows = pl.ds((r // d) * (ATT_BLK * d) + r % d, ATT_BLK, stride=d)
        for hp in range(N_PAIRS_A):
            sl = slice(hp * LANES, (hp + 1) * LANES)
            vcat = jnp.concatenate([vring_refs[g][slots[g], :, sl], v_ref[0, 0, :, sl]], axis=0)
            u0 = (g * N_PAIRS_A + hp) * 2
            o0 = _dot(p_ref[u0], vcat) * inv_ref[u0]
            o1 = _dot(p_ref[u0 + 1], vcat) * inv_ref[u0 + 1]
            acc_refs[g][hp, rows, :] = jnp.where(low_head, o0, o1)
            lse_refs[g][hp, rows, :] = jnp.where(low_head, stat_ref[u0], stat_ref[u0 + 1])
        kring_refs[g][slots[g]] = in_refs[3 * g + 1][0, 0]
        vring_refs[g][slots[g]] = v_ref[0, 0]

    @pl.when(r == n_res - 1)
    def _():
        chunk = 256
        for hp in range(A_WIDTH // LANES):
            def body(i, carry):
                rows = pl.ds(pl.multiple_of(i * chunk, chunk), chunk)
                lse = [ref[hp, rows, :] for ref in lse_refs]
                top = functools.reduce(jnp.maximum, lse)
                w = [jnp.exp(x - top) for x in lse]
                num = functools.reduce(jnp.add, [wi * ref[hp, rows, :] for wi, ref in zip(w, acc_refs)])
                out_ref[0, rows, hp * LANES:(hp + 1) * LANES] = (num / functools.reduce(jnp.add, w)).astype(BF16)
                return carry
            lax.fori_loop(0, ATT_SPAN // chunk, body, 0)


def _attention(qkv, batch, seq):
    n_res = DILATIONS[-1]
    n_span = seq // ATT_SPAN
    nd = len(DILATIONS)
    operands, in_specs = [], []
    for di, d in enumerate(DILATIONS):
        per = n_res // d

        def cur(b, n, r, d=d, per=per):
            return (r % d, b, n * per + r // d, 0)

        operands += [qkv[which * nd + di].reshape(d, batch, seq // d, A_WIDTH) for which in range(3)]
        in_specs += [pl.BlockSpec((1, 1, ATT_BLK, A_WIDTH), cur)] * 3
    bias = jnp.asarray(_attn_bias(), BF16)
    n_units = len(DILATIONS) * N_HEADS_A
    rings = [pltpu.VMEM((d, ATT_BLK, A_WIDTH), BF16) for d in DILATIONS]
    out = pl.pallas_call(
        _attn_kernel,
        grid=(batch, n_span, n_res),
        in_specs=in_specs + [_const_spec(bias.shape)],
        out_specs=pl.BlockSpec((1, ATT_SPAN, A_WIDTH), lambda b, n, r: (b, n, 0)),
        out_shape=jax.ShapeDtypeStruct((batch, seq, A_WIDTH), BF16),
        scratch_shapes=[pltpu.VMEM((N_PAIRS_A, ATT_SPAN, LANES), F32)] * (2 * len(DILATIONS))
        + [pltpu.VMEM((n_units, ATT_BLK, 2 * ATT_BLK), F32), pltpu.VMEM((n_units, ATT_BLK, 2 * ATT_BLK), BF16),
           pltpu.VMEM((n_units, ATT_BLK, LANES), F32), pltpu.VMEM((n_units, ATT_BLK, LANES), F32)]
        + rings + rings,
        compiler_params=_params(("arbitrary", "arbitrary", "arbitrary")),
        name="dilated_attention",
    )(*operands, bias)
    return out.reshape(batch * seq, A_WIDTH)


def _gla_kernel(q_ref, k_ref, la_ref, v_ref, r_ref, g_ref, o_ref, state_ref,
                qi_ref, ki_ref, kd_ref, qg_ref, dec_ref, kv_ref, sb_ref, oi_ref, *, n_chunks):
    @pl.when(pl.program_id(2) == 0)
    def _():
        state_ref[...] = jnp.zeros_like(state_ref)

    c = GLA_CHUNK
    two = 2 * c
    row2 = lax.broadcasted_iota(jnp.int32, (two, two), 0)
    col2 = lax.broadcasted_iota(jnp.int32, (two, two), 1)
    tril2 = jnp.where((col2 <= row2) & (col2 // c == row2 // c), 1.0, 0.0).astype(BF16)
    gain = g_ref[...]

    for gi in range(n_chunks // 2):
        rows = slice(gi * two, (gi + 1) * two)
        la = la_ref[rows, :]
        hi = la.astype(BF16)
        rem = la - hi.astype(F32)
        mid = rem.astype(BF16)
        lo = (rem - mid.astype(F32)).astype(BF16)
        b3 = _dot(tril2, jnp.concatenate([hi, mid, lo], axis=1))
        b = b3[:, :LANES] + b3[:, LANES:2 * LANES] + b3[:, 2 * LANES:]
        b_last = jnp.concatenate([jnp.broadcast_to(b[c - 1:c], (c, LANES)),
                                  jnp.broadcast_to(b[two - 1:two], (c, LANES))], axis=0)
        b_mid = jnp.concatenate([jnp.broadcast_to(b[c // 2 - 1:c // 2], (c, LANES)),
                                 jnp.broadcast_to(b[c + c // 2 - 1:c + c // 2], (c, LANES))], axis=0)
        q = q_ref[rows, :] * (DK_B ** -0.5)
        k = k_ref[rows, :]
        qi_ref[rows, :] = (q * jnp.exp(b - b_mid)).astype(BF16)
        ki_ref[rows, :] = (k * jnp.exp(b_mid - b)).astype(BF16)
        kd_ref[rows, :] = (k * jnp.exp(b_last - b)).astype(BF16)
        qg_ref[rows, :] = (q * jnp.exp(b)).astype(BF16)
        dec_ref[2 * gi:2 * gi + 1, :] = jnp.exp(b[c - 1:c])
        dec_ref[2 * gi + 1:2 * gi + 2, :] = jnp.exp(b[two - 1:two])

    lane_k = lax.broadcasted_iota(jnp.int32, (c, LANES), 1)
    lane_v = lax.broadcasted_iota(jnp.int32, (c, 2 * DV_B), 1)
    att_row = lax.broadcasted_iota(jnp.int32, (c, 2 * c), 0)
    att_col = lax.broadcasted_iota(jnp.int32, (c, 2 * c), 1)
    causal = (att_col % c) <= att_row
    inc_row = lax.broadcasted_iota(jnp.int32, (2 * DV_B, LANES), 0)
    inc_col = lax.broadcasted_iota(jnp.int32, (2 * DV_B, LANES), 1)
    own_block = (inc_row // DV_B) == (inc_col // DK_B)
    zero = jnp.zeros((), BF16)
    for ci in range(n_chunks):
        rows = slice(ci * c, (ci + 1) * c)
        ki = ki_ref[rows, :]
        v = v_ref[rows, :]
        k_stack = jnp.concatenate([jnp.where(lane_k < DK_B, ki, zero), jnp.where(lane_k >= DK_B, ki, zero)], axis=0)
        att = jnp.where(causal, _dot_nt(qi_ref[rows, :], k_stack), 0.0)
        v_blocks = jnp.concatenate([jnp.where(lane_v < DV_B, v, zero), jnp.where(lane_v >= DV_B, v, zero)], axis=0)
        oi_ref[ci] = _dot(att.astype(BF16), v_blocks)
        v_t = v.astype(F32).T.astype(BF16)
        kv_ref[ci] = jnp.where(own_block, _dot(v_t, kd_ref[rows, :]), 0.0)

    state = state_ref[...]
    for ci in range(n_chunks):
        sb_ref[ci] = state.astype(BF16)
        state = state * dec_ref[ci:ci + 1, :] + kv_ref[ci]
    state_ref[...] = state

    for ci in range(n_chunks):
        rows = slice(ci * c, (ci + 1) * c)
        o = oi_ref[ci] + _dot_nt(qg_ref[rows, :], sb_ref[ci])
        rg = r_ref[rows, :]
        gate = rg * jax.nn.sigmoid(rg)
        for h in range(2):
            cols = slice(h * DV_B, (h + 1) * DV_B)
            o_ref[rows, cols] = (_rms(o[:, cols]) * gain * gate[:, cols]).astype(BF16)


def _gla(qb, kb, la, vb, rb, g_out, batch, seq, tg=1024):
    t = batch * seq
    n_pairs = N_HEADS_B // 2
    per = seq // tg
    qk = pl.BlockSpec((tg, LANES), lambda b, p, i: (b * per + i, p))
    vr = pl.BlockSpec((tg, 2 * DV_B), lambda b, p, i: (b * per + i, p))
    n_chunks = tg // GLA_CHUNK
    return pl.pallas_call(
        functools.partial(_gla_kernel, n_chunks=n_chunks),
        grid=(batch, n_pairs, per),
        in_specs=[qk, qk, qk, vr, vr, _const_spec((1, DV_B))],
        out_specs=vr,
        out_shape=jax.ShapeDtypeStruct((t, B_V), BF16),
        scratch_shapes=[pltpu.VMEM((2 * DV_B, LANES), F32)]
        + [pltpu.VMEM((tg, LANES), BF16)] * 4
        + [pltpu.VMEM((n_chunks, LANES), F32), pltpu.VMEM((n_chunks, 2 * DV_B, LANES), F32),
           pltpu.VMEM((n_chunks, 2 * DV_B, LANES), BF16), pltpu.VMEM((n_chunks, GLA_CHUNK, 2 * DV_B), F32)],
        compiler_params=_params(("arbitrary", "arbitrary", "arbitrary")),
        name="gla",
    )(qb, kb, la, vb, rb, g_out)


def _sgu_kernel(h_ref, g_ref, w_in_ref, g_sgu_ref, ws_ref, bias_ref, o1_ref, o2_ref, hn_ref, v_ref, *, n_chunks):
    d = D_MODEL
    hn_ref[...] = (_rms(h_ref[...]) * g_ref[...]).astype(BF16)
    v = jax.nn.gelu(_dot(hn_ref[...], w_in_ref[:, d:]))
    v_ref[...] = (_rms(v) * g_sgu_ref[...]).astype(BF16)
    row = lax.broadcasted_iota(jnp.int32, (SGU_CHUNK, SGU_CHUNK), 0)
    col = lax.broadcasted_iota(jnp.int32, (SGU_CHUNK, SGU_CHUNK), 1)
    tril = col <= row
    half = d // 2
    per_dot = MXU_N // SGU_CHUNK
    for gg in range(N_GROUPS_C // per_dot):
        u_pair = jax.nn.gelu(_dot(hn_ref[...], w_in_ref[:, gg * MXU_N:(gg + 1) * MXU_N]))
        for gi in range(per_dot):
            g = gg * per_dot + gi
            ws = jnp.where(tril, ws_ref[g], 0.0).astype(BF16)
            cols = slice(g * SGU_CHUNK, (g + 1) * SGU_CHUNK)
            u_all = u_pair[:, gi * SGU_CHUNK:(gi + 1) * SGU_CHUNK]
            for cc in range(n_chunks // per_dot):
                chunks = range(cc * per_dot, (cc + 1) * per_dot)
                v_cat = jnp.concatenate([v_ref[ci * SGU_CHUNK:(ci + 1) * SGU_CHUNK, cols] for ci in chunks], axis=1)
                sv_cat = _dot(ws, v_cat)
                for k, ci in enumerate(chunks):
                    rows = slice(ci * SGU_CHUNK, (ci + 1) * SGU_CHUNK)
                    sv = sv_cat[:, k * SGU_CHUNK:(k + 1) * SGU_CHUNK] + bias_ref[:, cols]
                    y = (u_all[rows] * sv).astype(BF16)
                    if g < N_GROUPS_C // 2:
                        o1_ref[rows, cols] = y
                    else:
                        o2_ref[rows, g * SGU_CHUNK - half:(g + 1) * SGU_CHUNK - half] = y


def _sgu(h, g, w_in, g_sgu, ws, layer, bias, tm=512):
    t = h.shape[0]
    row = lambda n: pl.BlockSpec((tm, n), lambda i: (i, 0))
    half = D_MODEL // 2
    return pl.pallas_call(
        functools.partial(_sgu_kernel, n_chunks=tm // SGU_CHUNK),
        grid=(t // tm,),
        in_specs=[row(D_MODEL), _const_spec((1, D_MODEL)), _layer_spec(w_in, layer),
                  _const_spec((1, D_MODEL)), _layer_spec(ws, layer), _const_spec(bias.shape)],
        out_specs=[row(half), row(half)],
        out_shape=[jax.ShapeDtypeStruct((t, half), BF16)] * 2,
        scratch_shapes=[pltpu.VMEM((tm, D_MODEL), BF16), pltpu.VMEM((tm, D_MODEL), BF16)],
        compiler_params=_params(("arbitrary",)),
        name="sgu",
    )(h, g, w_in, g_sgu, ws, bias)


def _shift_rows(z, prev, k):
    rolled = pltpu.roll(z, k, axis=0)
    head = jnp.where(lax.broadcasted_iota(jnp.int32, prev.shape, 0) < k,
                     pltpu.roll(prev, k, axis=0), rolled[:8])
    return jnp.concatenate([head, rolled[8:]], axis=0)


def _ffn_kernel(h_ref, o1_ref, o2_ref, w_o_ref, g_ref, w_up_ref, cw_ref, cb_ref, w_down_ref, gf_ref,
                out_ref, hn_ref, act_ref, carry_ref, *, final_norm):
    @pl.when(pl.program_id(1) == 0)
    def _():
        carry_ref[...] = jnp.zeros_like(carry_ref)

    tm = h_ref.shape[0]
    o = jnp.concatenate([o1_ref[...], o2_ref[...]], axis=1)
    h1 = h_ref[...] + _dot(o, w_o_ref[...])
    out_ref[...] = h1
    hn_ref[...] = (_rms(h1) * g_ref[...]).astype(BF16)

    def conv(idx, off):
        z = _dot(hn_ref[...], w_up_ref[:, off:off + FF_CHUNK])
        prev = carry_ref[idx]
        carry_ref[idx] = z[tm - 8:]
        w = cw_ref[:, off:off + FF_CHUNK]
        acc = _shift_rows(z, prev, 2) * w[0:1] + _shift_rows(z, prev, 1) * w[1:2] + z * w[2:3]
        return acc + cb_ref[:, off:off + FF_CHUNK]

    for c in range(N_FF_CHUNKS):
        gate = conv(c, c * FF_CHUNK)
        up = conv(N_FF_CHUNKS + c, D_FF + c * FF_CHUNK)
        act_ref[:, c * FF_CHUNK:(c + 1) * FF_CHUNK] = (gate * jax.nn.sigmoid(gate) * up).astype(BF16)
    h2 = out_ref[...] + _dot(act_ref[...], w_down_ref[...])
    if final_norm:
        h2 = _rms(h2) * gf_ref[...]
    out_ref[...] = h2


def _ffn(h, o1, o2, w_o, mixer_layer, g, w_up, conv_w, conv_b, w_down, layer, g_final, batch, seq, final_norm,
         tm=1024):
    t = batch * seq
    per = seq // tm
    row = lambda n: pl.BlockSpec((tm, n), lambda b, i: (b * per + i, 0))
    return pl.pallas_call(
        functools.partial(_ffn_kernel, final_norm=final_norm),
        grid=(batch, per),
        in_specs=[row(D_MODEL), row(D_MODEL // 2), row(D_MODEL // 2), _layer_spec(w_o, mixer_layer),
                  _const_spec((1, D_MODEL)), _layer_spec(w_up, layer), _layer_spec(conv_w, layer),
                  _const_spec(conv_b.shape), _layer_spec(w_down, layer), _const_spec((1, D_MODEL))],
        out_specs=row(D_MODEL),
        out_shape=jax.ShapeDtypeStruct((t, D_MODEL), F32),
        scratch_shapes=[pltpu.VMEM((tm, D_MODEL), BF16), pltpu.VMEM((tm, D_FF), BF16),
                        pltpu.VMEM((2 * N_FF_CHUNKS, 8, FF_CHUNK), F32)],
        compiler_params=_params(("arbitrary", "arbitrary")),
        name="ffn",
    )(h, o1, o2, w_o, g, w_up, conv_w, conv_b, w_down, g_final)


def kernel(x, positions, norm_mix, norm_ffn, w_in_ab, w_gate_up, b_gate_up, g_out_b, w_out_ab, w_in_c,
           g_sgu, w_spatial, b_spatial, w_out_c, w_up, conv_w, conv_b, w_down, norm_final):
    batch, seq, d = x.shape
    depth = norm_mix.shape[0]
    t = batch * seq
    h = x.reshape(t, d)
    cos, sin = _rope_tables(positions)
    row = lambda v: v.reshape(1, -1)
    b_end = 3 * A_WIDTH + 2 * B_QK + 2 * B_V
    w_in_ab_b, w_in_c_b = w_in_ab.astype(BF16), w_in_c.astype(BF16)
    w_out_b = (w_out_ab.astype(BF16), w_out_c.astype(BF16))
    w_up_b, w_down_b = w_up.astype(BF16), w_down.astype(BF16)
    for layer in range(depth):
        i = layer // 2
        if layer % 2 == 0:
            wg = jnp.pad(w_in_ab[i][:, b_end:], ((0, 0), (0, LANES - GATE_RANK))).astype(BF16)
            wgu = jnp.pad(w_gate_up[i], ((0, LANES - GATE_RANK), (0, 0))).astype(BF16)
            qkv, (qb, kb, vb, rb, la) = _proj_ab(
                h, row(norm_mix[layer]), w_in_ab_b, i, wg, wgu, row(b_gate_up[i]), cos, sin)
            o1 = _attention(qkv, batch, seq)
            o2 = _gla(qb, kb, la, vb, rb, row(g_out_b[i]), batch, seq)
        else:
            bias = jnp.repeat(b_spatial[i].T, SGU_CHUNK, axis=1)
            o1, o2 = _sgu(h, row(norm_mix[layer]), w_in_c_b, row(g_sgu[i]), w_spatial, i, bias)
        h = _ffn(h, o1, o2, w_out_b[layer % 2], i, row(norm_ffn[layer]), w_up_b, conv_w, row(conv_b[layer]),
                 w_down_b, layer, row(norm_final), batch, seq, final_norm=(layer == depth - 1))
    return h.reshape(batch, seq, d)
```

```python
import functools

import numpy as np
import jax
import jax.numpy as jnp
from jax import lax
from jax.experimental import pallas as pl
from jax.experimental.pallas import tpu as pltpu

F32 = jnp.float32
BF16 = jnp.bfloat16

D_MODEL = 1024
N_HEADS_A = 8
HEAD_DIM_A = 64
ROT_DIM = HEAD_DIM_A // 4
ROPE_THETA = 500000.0
DILATIONS = (1, 4, 16)
ATT_BLK = 128
ATT_SPAN = ATT_BLK * DILATIONS[-1]
N_HEADS_B = 4
DV_B = 128
DK_B = 64
GATE_RANK = 16
GATE_TAU = 16.0
GLA_CHUNK = 64
N_GROUPS_C = 8
SGU_CHUNK = 128
D_FF = 2816
CONV_W = 3
A_WIDTH = N_HEADS_A * HEAD_DIM_A
B_QK = N_HEADS_B * DK_B
B_V = N_HEADS_B * DV_B
NEG = -1e30
EPS = 1e-6
LANES = 128
N_PAIRS_A = A_WIDTH // LANES
MXU_N = 256
FF_CHUNK = MXU_N
N_FF_CHUNKS = D_FF // FF_CHUNK

VMEM_LIMIT = 56 * 1024 * 1024


def _dot(a, b):
    return jnp.dot(a, b, preferred_element_type=F32)


def _dot_nt(a, b):
    return lax.dot_general(a, b, (((1,), (1,)), ((), ())), preferred_element_type=F32)


def _rms(x):
    return x * lax.rsqrt(jnp.mean(x * x, axis=-1, keepdims=True) + EPS)


def _const_spec(shape):
    nd = len(shape)
    return pl.BlockSpec(shape, lambda *_: (0,) * nd, pipeline_mode=pl.Buffered(1))


def _layer_spec(stacked, layer):
    nd = stacked.ndim - 1
    return pl.BlockSpec((None,) + stacked.shape[1:], lambda *_: (layer,) + (0,) * nd, pipeline_mode=pl.Buffered(1))


def _params(sem, vmem=VMEM_LIMIT):
    return pltpu.CompilerParams(dimension_semantics=sem, vmem_limit_bytes=vmem)


def _rope_tab_kernel(pos_ref, inv_ref, sgn_ref, c_ref, s_ref):
    ang = pos_ref[...].astype(F32) * inv_ref[...]
    c_ref[...] = jnp.cos(ang)
    s_ref[...] = sgn_ref[...] * jnp.sin(ang)


def _rope_tables(positions):
    t = positions.size
    half = ROT_DIM // 2
    inv = np.float64(ROPE_THETA) ** (-np.arange(half, dtype=np.float64) * (2.0 / ROT_DIM))
    dim = np.arange(LANES) % HEAD_DIM_A
    inv_lane = np.where(dim < ROT_DIM, inv[dim % half], 0.0).astype(np.float32)[None, :]
    sgn_lane = np.where(dim < half, -1.0, np.where(dim < ROT_DIM, 1.0, 0.0)).astype(np.float32)[None, :]
    tm = 1024
    pos = positions.reshape(t, 1)
    return pl.pallas_call(
        _rope_tab_kernel,
        grid=(t // tm,),
        in_specs=[pl.BlockSpec((tm, 1), lambda i: (i, 0)),
                  _const_spec((1, LANES)), _const_spec((1, LANES))],
        out_specs=[pl.BlockSpec((tm, LANES), lambda i: (i, 0))] * 2,
        out_shape=[jax.ShapeDtypeStruct((t, LANES), F32)] * 2,
        compiler_params=_params(("arbitrary",)),
        name="rope_tables",
    )(pos, jnp.asarray(inv_lane), jnp.asarray(sgn_lane))


def _proj_ab_kernel(h_ref, g_ref, w_ref, wg_ref, wgu_ref, bgu_ref, c_ref, s_ref, *refs):
    nd = len(DILATIONS)
    qkv_refs = refs[:3 * nd]
    qb_ref, kb_ref, vb_ref, rb_ref, la_ref, hn_ref, stage_ref = refs[3 * nd:]
    tm = h_ref.shape[0]
    hn_ref[...] = (_rms(h_ref[...]) * g_ref[...]).astype(BF16)
    cos = c_ref[...]
    sin = s_ref[...]
    lane = lax.broadcasted_iota(jnp.int32, cos.shape, 1)
    first_half = (lane % HEAD_DIM_A) < (ROT_DIM // 2)

    def rope(x):
        partner = jnp.where(first_half,
                            pltpu.roll(x, LANES - ROT_DIM // 2, axis=1),
                            pltpu.roll(x, ROT_DIM // 2, axis=1))
        return x * cos + partner * sin

    for which in range(3):
        for jj in range(A_WIDTH // MXU_N):
            col = which * A_WIDTH + jj * MXU_N
            x2 = _dot(hn_ref[...], w_ref[:, col:col + MXU_N])
            for half in range(MXU_N // LANES):
                j = jj * (MXU_N // LANES) + half
                x = x2[:, half * LANES:(half + 1) * LANES]
                if which < 2:
                    x = rope(x)
                qkv_refs[which * nd][:, j * LANES:(j + 1) * LANES] = x.astype(BF16)
                stage = stage_ref.at[which * N_PAIRS_A + j]
                stage[...] = x
                for di, d in enumerate(DILATIONS[1:], start=1):
                    for r in range(d):
                        piece = stage[pl.ds(r, tm // d, stride=d), :]
                        qkv_refs[which * nd + di][r, :, j * LANES:(j + 1) * LANES] = piece.astype(BF16)
    b0 = 3 * A_WIDTH
    qb_ref[...] = _dot(hn_ref[...], w_ref[:, b0:b0 + B_QK])
    kb_ref[...] = _dot(hn_ref[...], w_ref[:, b0 + B_QK:b0 + 2 * B_QK])
    vb_ref[...] = _dot(hn_ref[...], w_ref[:, b0 + 2 * B_QK:b0 + 2 * B_QK + B_V]).astype(BF16)
    rb_ref[...] = _dot(hn_ref[...], w_ref[:, b0 + 2 * B_QK + B_V:b0 + 2 * B_QK + 2 * B_V])
    gl = _dot(hn_ref[...], wg_ref[...])
    g = _dot(gl.astype(BF16), wgu_ref[...]) + bgu_ref[...]
    la_ref[...] = (jnp.minimum(g, 0.0) - jnp.log1p(jnp.exp(-jnp.abs(g)))) * (1.0 / GATE_TAU)


def _proj_ab(h, g, w_in, layer, wg, wgu, bgu, cos, sin, tm=512):
    t = h.shape[0]
    row = lambda n: pl.BlockSpec((tm, n), lambda i: (i, 0))
    specs = [row(A_WIDTH)] + [pl.BlockSpec((d, tm // d, A_WIDTH), lambda i: (0, i, 0)) for d in DILATIONS[1:]]
    shapes = [jax.ShapeDtypeStruct((t, A_WIDTH), BF16)]
    shapes += [jax.ShapeDtypeStruct((d, t // d, A_WIDTH), BF16) for d in DILATIONS[1:]]
    specs, shapes = specs * 3, shapes * 3
    out_widths = (B_QK, B_QK, B_V, B_V, B_QK)
    out_dtypes = (F32, F32, BF16, F32, F32)
    outs = pl.pallas_call(
        _proj_ab_kernel,
        grid=(t // tm,),
        in_specs=[row(D_MODEL), _const_spec((1, D_MODEL)), _layer_spec(w_in, layer),
                  _const_spec(wg.shape), _const_spec(wgu.shape), _const_spec(bgu.shape),
                  row(LANES), row(LANES)],
        out_specs=specs + [row(n) for n in out_widths],
        out_shape=shapes + [jax.ShapeDtypeStruct((t, n), d) for n, d in zip(out_widths, out_dtypes)],
        scratch_shapes=[pltpu.VMEM((tm, D_MODEL), BF16), pltpu.VMEM((3 * N_PAIRS_A, tm, LANES), F32)],
        compiler_params=_params(("arbitrary",)),
        name="proj_ab",
    )(h, g, w_in, wg, wgu, bgu, cos, sin)
    return outs[:3 * len(DILATIONS)], outs[3 * len(DILATIONS):]


def _attn_bias():
    a = np.arange(ATT_BLK)[None, :]
    c = np.arange(2 * ATT_BLK)[:, None]
    band = (c >= a) & (c <= a + ATT_BLK)
    first = band & (c >= ATT_BLK)
    return np.where(np.stack([band, first]), 0.0, NEG).astype(np.float32)


def _attn_kernel(*refs):
    nb = len(DILATIONS)
    in_refs = refs[:3 * nb]
    bias_ref = refs[3 * nb]
    out_ref = refs[3 * nb + 1]
    scratch = refs[3 * nb + 2:]
    acc_refs, lse_refs = scratch[:nb], scratch[nb:2 * nb]
    s_ref, p_ref, inv_ref, stat_ref = scratch[2 * nb:2 * nb + 4]
    kring_refs, vring_refs = scratch[2 * nb + 4:3 * nb + 4], scratch[3 * nb + 4:]
    n = pl.program_id(1)
    r = pl.program_id(2)
    n_res = DILATIONS[-1]
    lane = lax.broadcasted_iota(jnp.int32, (ATT_BLK, LANES), 1)
    low_head = lane < HEAD_DIM_A
    zero = jnp.zeros((), BF16)
    scale = jnp.asarray(HEAD_DIM_A ** -0.5, BF16)
    eye = jnp.where(lax.broadcasted_iota(jnp.int32, (ATT_BLK, ATT_BLK), 0)
                    == lax.broadcasted_iota(jnp.int32, (ATT_BLK, ATT_BLK), 1), 1.0, 0.0).astype(BF16)
    slots = [r % d for d in DILATIONS]
    blks = [n * (n_res // d) + r // d for d in DILATIONS]

    @pl.when(n == 0)
    def _():
        for g in range(nb):
            @pl.when(blks[g] == 0)
            def _():
                kring_refs[g][slots[g]] = jnp.zeros((ATT_BLK, A_WIDTH), BF16)
                vring_refs[g][slots[g]] = jnp.zeros((ATT_BLK, A_WIDTH), BF16)

    for g, d in enumerate(DILATIONS):
        q_ref, k_ref = in_refs[3 * g:3 * g + 2]
        bias_t = bias_ref[jnp.where(blks[g] == 0, 1, 0)]
        for hp in range(N_PAIRS_A):
            sl = slice(hp * LANES, (hp + 1) * LANES)
            q2 = q_ref[0, 0, :, sl] * scale
            kcat = jnp.concatenate([kring_refs[g][slots[g], :, sl], k_ref[0, 0, :, sl]], axis=0)
            rhs = jnp.concatenate([kcat, bias_t], axis=1)
            for hh in range(2):
                qm = jnp.where(low_head == (hh == 0), q2, zero)
                s_ref[(g * N_PAIRS_A + hp) * 2 + hh] = _dot_nt(jnp.concatenate([qm, eye], axis=1), rhs)

    for u in range(nb * N_PAIRS_A * 2):
        s = s_ref[u]
        m = jnp.max(s, axis=-1, keepdims=True)
        p = jnp.exp(s - m)
        l = jnp.sum(p, axis=-1, keepdims=True)
        p_ref[u] = p.astype(BF16)
        inv_ref[u] = jnp.broadcast_to(1.0 / l, (ATT_BLK, LANES))
        stat_ref[u] = jnp.broadcast_to(m + jnp.log(l), (ATT_BLK, LANES))

    for g, d in enumerate(DILATIONS):
        v_ref = in_refs[3 * g + 2]
        if d == 1:
            rows = pl.ds(pl.multiple_of(r * ATT_BLK, ATT_BLK), ATT_BLK)
        else:
            rows = pl.ds((r // d) * (ATT_BLK * d) + r % d, ATT_BLK, stride=d)
        for hp in range(N_PAIRS_A):
            sl = slice(hp * LANES, (hp + 1) * LANES)
            vcat = jnp.concatenate([vring_refs[g][slots[g], :, sl], v_ref[0, 0, :, sl]], axis=0)
            u0 = (g * N_PAIRS_A + hp) * 2
            o0 = _dot(p_ref[u0], vcat) * inv_ref[u0]
            o1 = _dot(p_ref[u0 + 1], vcat) * inv_ref[u0 + 1]
            acc_refs[g][hp, rows, :] = jnp.where(low_head, o0, o1)
            lse_refs[g][hp, rows, :] = jnp.where(low_head, stat_ref[u0], stat_ref[u0 + 1])
        kring_refs[g][slots[g]] = in_refs[3 * g + 1][0, 0]
        vring_refs[g][slots[g]] = v_ref[0, 0]

    @pl.when(r == n_res - 1)
    def _():
        chunk = 256
        for hp in range(A_WIDTH // LANES):
            def body(i, carry):
                rows = pl.ds(pl.multiple_of(i * chunk, chunk), chunk)
                lse = [ref[hp, rows, :] for ref in lse_refs]
                top = functools.reduce(jnp.maximum, lse)
                w = [jnp.exp(x - top) for x in lse]
                num = functools.reduce(jnp.add, [wi * ref[hp, rows, :] for wi, ref in zip(w, acc_refs)])
                out_ref[0, rows, hp * LANES:(hp + 1) * LANES] = (num / functools.reduce(jnp.add, w)).astype(BF16)
                return carry
            lax.fori_loop(0, ATT_SPAN // chunk, body, 0)


def _attention(qkv, batch, seq):
    n_res = DILATIONS[-1]
    n_span = seq // ATT_SPAN
    nd = len(DILATIONS)
    operands, in_specs = [], []
    for di, d in enumerate(DILATIONS):
        per = n_res // d

        def cur(b, n, r, d=d, per=per):
            return (r % d, b, n * per + r // d, 0)

        operands += [qkv[which * nd + di].reshape(d, batch, seq // d, A_WIDTH) for which in range(3)]
        in_specs += [pl.BlockSpec((1, 1, ATT_BLK, A_WIDTH), cur)] * 3
    bias = jnp.asarray(_attn_bias(), BF16)
    n_units = len(DILATIONS) * N_HEADS_A
    rings = [pltpu.VMEM((d, ATT_BLK, A_WIDTH), BF16) for d in DILATIONS]
    out = pl.pallas_call(
        _attn_kernel,
        grid=(batch, n_span, n_res),
        in_specs=in_specs + [_const_spec(bias.shape)],
        out_specs=pl.BlockSpec((1, ATT_SPAN, A_WIDTH), lambda b, n, r: (b, n, 0)),
        out_shape=jax.ShapeDtypeStruct((batch, seq, A_WIDTH), BF16),
        scratch_shapes=[pltpu.VMEM((N_PAIRS_A, ATT_SPAN, LANES), F32)] * (2 * len(DILATIONS))
        + [pltpu.VMEM((n_units, ATT_BLK, 2 * ATT_BLK), F32), pltpu.VMEM((n_units, ATT_BLK, 2 * ATT_BLK), BF16),
           pltpu.VMEM((n_units, ATT_BLK, LANES), F32), pltpu.VMEM((n_units, ATT_BLK, LANES), F32)]
        + rings + rings,
        compiler_params=_params(("arbitrary", "arbitrary", "arbitrary")),
        name="dilated_attention",
    )(*operands, bias)
    return out.reshape(batch * seq, A_WIDTH)


def _gla_kernel(q_ref, k_ref, la_ref, v_ref, r_ref, g_ref, o_ref, state_ref,
                qi_ref, ki_ref, kd_ref, qg_ref, dec_ref, kv_ref, sb_ref, oi_ref, att_ref, *, n_chunks):
    @pl.when(pl.program_id(2) == 0)
    def _():
        state_ref[...] = jnp.zeros_like(state_ref)

    c = GLA_CHUNK
    two = 2 * c
    row2 = lax.broadcasted_iota(jnp.int32, (two, two), 0)
    col2 = lax.broadcasted_iota(jnp.int32, (two, two), 1)
    tril2 = jnp.where((col2 <= row2) & (col2 // c == row2 // c), 1.0, 0.0).astype(BF16)
    gain = g_ref[...]

    for gi in range(n_chunks // 2):
        rows = slice(gi * two, (gi + 1) * two)
        la = la_ref[rows, :]
        hi = la.astype(BF16)
        rem = la - hi.astype(F32)
        mid = rem.astype(BF16)
        lo = (rem - mid.astype(F32)).astype(BF16)
        b3 = _dot(tril2, jnp.concatenate([hi, mid, lo], axis=1))
        b = b3[:, :LANES] + b3[:, LANES:2 * LANES] + b3[:, 2 * LANES:]
        b_last = jnp.concatenate([jnp.broadcast_to(b[c - 1:c], (c, LANES)),
                                  jnp.broadcast_to(b[two - 1:two], (c, LANES))], axis=0)
        b_mid = jnp.concatenate([jnp.broadcast_to(b[c // 2 - 1:c // 2], (c, LANES)),
                                 jnp.broadcast_to(b[c + c // 2 - 1:c + c // 2], (c, LANES))], axis=0)
        q = q_ref[rows, :] * (DK_B ** -0.5)
        k = k_ref[rows, :]
        qi_ref[rows, :] = (q * jnp.exp(b - b_mid)).astype(BF16)
        ki_ref[rows, :] = (k * jnp.exp(b_mid - b)).astype(BF16)
        kd_ref[rows, :] = (k * jnp.exp(b_last - b)).astype(BF16)
        qg_ref[rows, :] = (q * jnp.exp(b)).astype(BF16)
        dec_ref[2 * gi:2 * gi + 1, :] = jnp.exp(b[c - 1:c])
        dec_ref[2 * gi + 1:2 * gi + 2, :] = jnp.exp(b[two - 1:two])

    lane_k = lax.broadcasted_iota(jnp.int32, (c, LANES), 1)
    lane_v = lax.broadcasted_iota(jnp.int32, (c, 2 * DV_B), 1)
    att_row = lax.broadcasted_iota(jnp.int32, (c, 2 * c), 0)
    att_col = lax.broadcasted_iota(jnp.int32, (c, 2 * c), 1)
    causal = (att_col % c) <= att_row
    inc_row = lax.broadcasted_iota(jnp.int32, (2 * DV_B, LANES), 0)
    inc_col = lax.broadcasted_iota(jnp.int32, (2 * DV_B, LANES), 1)
    own_block = (inc_row // DV_B) == (inc_col // DK_B)
    zero = jnp.zeros((), BF16)
    for ci in range(n_chunks):
        rows = slice(ci * c, (ci + 1) * c)
        ki = ki_ref[rows, :]
        k_stack = jnp.concatenate([jnp.where(lane_k < DK_B, ki, zero), jnp.where(lane_k >= DK_B, ki, zero)], axis=0)
        att = jnp.where(causal, _dot_nt(qi_ref[rows, :], k_stack), 0.0)
        att_ref[ci] = att.astype(BF16)
        v_t = v_ref[rows, :].astype(F32).T.astype(BF16)
        kv_ref[ci] = jnp.where(own_block, _dot(v_t, kd_ref[rows, :]), 0.0)
    for ci in range(n_chunks):
        v = v_ref[ci * c:(ci + 1) * c, :]
        v_blocks = jnp.concatenate([jnp.where(lane_v < DV_B, v, zero), jnp.where(lane_v >= DV_B, v, zero)], axis=0)
        oi_ref[ci] = _dot(att_ref[ci], v_blocks)

    state = state_ref[...]
    for ci in range(n_chunks):
        sb_ref[ci] = state.astype(BF16)
        state = state * dec_ref[ci:ci + 1, :] + kv_ref[ci]
    state_ref[...] = state

    for ci in range(n_chunks):
        rows = slice(ci * c, (ci + 1) * c)
        o = oi_ref[ci] + _dot_nt(qg_ref[rows, :], sb_ref[ci])
        rg = r_ref[rows, :]
        gate = rg * jax.nn.sigmoid(rg)
        for h in range(2):
            cols = slice(h * DV_B, (h + 1) * DV_B)
            o_ref[rows, cols] = (_rms(o[:, cols]) * gain * gate[:, cols]).astype(BF16)


def _gla(qb, kb, la, vb, rb, g_out, batch, seq, tg=1024):
    t = batch * seq
    n_pairs = N_HEADS_B // 2
    per = seq // tg
    qk = pl.BlockSpec((tg, LANES), lambda b, p, i: (b * per + i, p))
    vr = pl.BlockSpec((tg, 2 * DV_B), lambda b, p, i: (b * per + i, p))
    n_chunks = tg // GLA_CHUNK
    return pl.pallas_call(
        functools.partial(_gla_kernel, n_chunks=n_chunks),
        grid=(batch, n_pairs, per),
        in_specs=[qk, qk, qk, vr, vr, _const_spec((1, DV_B))],
        out_specs=vr,
        out_shape=jax.ShapeDtypeStruct((t, B_V), BF16),
        scratch_shapes=[pltpu.VMEM((2 * DV_B, LANES), F32)]
        + [pltpu.VMEM((tg, LANES), BF16)] * 4
        + [pltpu.VMEM((n_chunks, LANES), F32), pltpu.VMEM((n_chunks, 2 * DV_B, LANES), F32),
           pltpu.VMEM((n_chunks, 2 * DV_B, LANES), BF16), pltpu.VMEM((n_chunks, GLA_CHUNK, 2 * DV_B), F32),
           pltpu.VMEM((n_chunks, GLA_CHUNK, 2 * GLA_CHUNK), BF16)],
        compiler_params=_params(("arbitrary", "arbitrary", "arbitrary")),
        name="gla",
    )(qb, kb, la, vb, rb, g_out)


def _sgu_kernel(h_ref, g_ref, w_in_ref, g_sgu_ref, ws_ref, bias_ref, o1_ref, o2_ref, hn_ref, v_ref, u_ref, *,
                n_chunks):
    d = D_MODEL
    hn_ref[...] = (_rms(h_ref[...]) * g_ref[...]).astype(BF16)
    v = jax.nn.gelu(_dot(hn_ref[...], w_in_ref[:, d:]))
    for cc in range(d // MXU_N):
        cols = slice(cc * MXU_N, (cc + 1) * MXU_N)
        u_ref[:, cols] = jax.nn.gelu(_dot(hn_ref[...], w_in_ref[:, cols]))
    v_ref[...] = (_rms(v) * g_sgu_ref[...]).astype(BF16)
    row = lax.broadcasted_iota(jnp.int32, (SGU_CHUNK, SGU_CHUNK), 0)
    col = lax.broadcasted_iota(jnp.int32, (SGU_CHUNK, SGU_CHUNK), 1)
    tril = col <= row
    half = d // 2
    per_dot = MXU_N // SGU_CHUNK
    for g in range(N_GROUPS_C):
        ws = jnp.where(tril, ws_ref[g], 0.0).astype(BF16)
        cols = slice(g * SGU_CHUNK, (g + 1) * SGU_CHUNK)
        for cc in range(n_chunks // per_dot):
            chunks = range(cc * per_dot, (cc + 1) * per_dot)
            v_cat = jnp.concatenate([v_ref[ci * SGU_CHUNK:(ci + 1) * SGU_CHUNK, cols] for ci in chunks], axis=1)
            sv_cat = _dot(ws, v_cat)
            for k, ci in enumerate(chunks):
                rows = slice(ci * SGU_CHUNK, (ci + 1) * SGU_CHUNK)
                sv = sv_cat[:, k * SGU_CHUNK:(k + 1) * SGU_CHUNK] + bias_ref[:, cols]
                y = (u_ref[rows, cols] * sv).astype(BF16)
                if g < N_GROUPS_C // 2:
                    o1_ref[rows, cols] = y
                else:
                    o2_ref[rows, g * SGU_CHUNK - half:(g + 1) * SGU_CHUNK - half] = y


def _sgu(h, g, w_in, g_sgu, ws, layer, bias, tm=512):
    t = h.shape[0]
    row = lambda n: pl.BlockSpec((tm, n), lambda i: (i, 0))
    half = D_MODEL // 2
    return pl.pallas_call(
        functools.partial(_sgu_kernel, n_chunks=tm // SGU_CHUNK),
        grid=(t // tm,),
        in_specs=[row(D_MODEL), _const_spec((1, D_MODEL)), _layer_spec(w_in, layer),
                  _const_spec((1, D_MODEL)), _layer_spec(ws, layer), _const_spec(bias.shape)],
        out_specs=[row(half), row(half)],
        out_shape=[jax.ShapeDtypeStruct((t, half), BF16)] * 2,
        scratch_shapes=[pltpu.VMEM((tm, D_MODEL), BF16), pltpu.VMEM((tm, D_MODEL), BF16),
                        pltpu.VMEM((tm, D_MODEL), F32)],
        compiler_params=_params(("arbitrary",)),
        name="sgu",
    )(h, g, w_in, g_sgu, ws, bias)


def _shift_rows(z, prev, k):
    rolled = pltpu.roll(z, k, axis=0)
    head = jnp.where(lax.broadcasted_iota(jnp.int32, prev.shape, 0) < k,
                     pltpu.roll(prev, k, axis=0), rolled[:8])
    return jnp.concatenate([head, rolled[8:]], axis=0)


def _ffn_kernel(h_ref, o1_ref, o2_ref, w_o_ref, g_ref, w_up_ref, cw_ref, cb_ref, w_down_ref, gf_ref,
                out_ref, hn_ref, act_ref, carry_ref, *, final_norm):
    @pl.when(pl.program_id(1) == 0)
    def _():
        carry_ref[...] = jnp.zeros_like(carry_ref)

    tm = h_ref.shape[0]
    o = jnp.concatenate([o1_ref[...], o2_ref[...]], axis=1)
    h1 = h_ref[...] + _dot(o, w_o_ref[...])
    out_ref[...] = h1
    hn_ref[...] = (_rms(h1) * g_ref[...]).astype(BF16)

    def conv(c, part):
        off = part * D_FF + c * FF_CHUNK
        idx = part * N_FF_CHUNKS + c
        z = _dot(hn_ref[...], w_up_ref[:, off:off + FF_CHUNK])
        prev = carry_ref[idx]
        carry_ref[idx] = z[tm - 8:]
        w = cw_ref[:, off:off + FF_CHUNK]
        acc = _shift_rows(z, prev, 2) * w[0:1] + _shift_rows(z, prev, 1) * w[1:2] + z * w[2:3]
        return acc + cb_ref[:, off:off + FF_CHUNK]

    for c in range(N_FF_CHUNKS):
        gate = conv(c, 0)
        up = conv(c, 1)
        act_ref[:, c * FF_CHUNK:(c + 1) * FF_CHUNK] = (gate * jax.nn.sigmoid(gate) * up).astype(BF16)
    h2 = out_ref[...] + _dot(act_ref[...], w_down_ref[...])
    if final_norm:
        h2 = _rms(h2) * gf_ref[...]
    out_ref[...] = h2


def _ffn(h, o1, o2, w_o, mixer_layer, g, w_up, conv_w, conv_b, w_down, layer, g_final, batch, seq, final_norm,
         tm=1024):
    t = batch * seq
    per = seq // tm
    row = lambda n: pl.BlockSpec((tm, n), lambda b, i: (b * per + i, 0))
    return pl.pallas_call(
        functools.partial(_ffn_kernel, final_norm=final_norm),
        grid=(batch, per),
        in_specs=[row(D_MODEL), row(D_MODEL // 2), row(D_MODEL // 2), _layer_spec(w_o, mixer_layer),
                  _const_spec((1, D_MODEL)), _layer_spec(w_up, layer), _layer_spec(conv_w, layer),
                  _const_spec(conv_b.shape), _layer_spec(w_down, layer), _const_spec((1, D_MODEL))],
        out_specs=row(D_MODEL),
        out_shape=jax.ShapeDtypeStruct((t, D_MODEL), F32),
        scratch_shapes=[pltpu.VMEM((tm, D_MODEL), BF16), pltpu.VMEM((tm, D_FF), BF16),
                        pltpu.VMEM((2 * N_FF_CHUNKS, 8, FF_CHUNK), F32)],
        compiler_params=_params(("arbitrary", "arbitrary")),
        name="ffn",
    )(h, o1, o2, w_o, g, w_up, conv_w, conv_b, w_down, g_final)


def kernel(x, positions, norm_mix, norm_ffn, w_in_ab, w_gate_up, b_gate_up, g_out_b, w_out_ab, w_in_c,
           g_sgu, w_spatial, b_spatial, w_out_c, w_up, conv_w, conv_b, w_down, norm_final):
    batch, seq, d = x.shape
    depth = norm_mix.shape[0]
    t = batch * seq
    h = x.reshape(t, d)
    cos, sin = _rope_tables(positions)
    row = lambda v: v.reshape(1, -1)
    b_end = 3 * A_WIDTH + 2 * B_QK + 2 * B_V
    w_in_ab_b, w_in_c_b = w_in_ab.astype(BF16), w_in_c.astype(BF16)
    w_out_b = (w_out_ab.astype(BF16), w_out_c.astype(BF16))
    w_up_b, w_down_b = w_up.astype(BF16), w_down.astype(BF16)
    for layer in range(depth):
        i = layer // 2
        if layer % 2 == 0:
            wg = jnp.pad(w_in_ab[i][:, b_end:], ((0, 0), (0, LANES - GATE_RANK))).astype(BF16)
            wgu = jnp.pad(w_gate_up[i], ((0, LANES - GATE_RANK), (0, 0))).astype(BF16)
            qkv, (qb, kb, vb, rb, la) = _proj_ab(
                h, row(norm_mix[layer]), w_in_ab_b, i, wg, wgu, row(b_gate_up[i]), cos, sin)
            o1 = _attention(qkv, batch, seq)
            o2 = _gla(qb, kb, la, vb, rb, row(g_out_b[i]), batch, seq)
        else:
            bias = jnp.repeat(b_spatial[i].T, SGU_CHUNK, axis=1)
            o1, o2 = _sgu(h, row(norm_mix[layer]), w_in_c_b, row(g_sgu[i]), w_spatial, i, bias)
        h = _ffn(h, o1, o2, w_out_b[layer % 2], i, row(norm_ffn[layer]), w_up_b, conv_w, row(conv_b[layer]),
                 w_down_b, layer, row(norm_final), batch, seq, final_norm=(layer == depth - 1))
    return h.reshape(batch, seq, d)
```

```python
import functools

import numpy as np
import jax
import jax.numpy as jnp
from jax import lax
from jax.experimental import pallas as pl
from jax.experimental.pallas import tpu as pltpu

F32 = jnp.float32
BF16 = jnp.bfloat16

D_MODEL = 1024
N_HEADS_A = 8
HEAD_DIM_A = 64
ROT_DIM = HEAD_DIM_A // 4
ROPE_THETA = 500000.0
DILATIONS = (1, 4, 16)
ATT_BLK = 128
ATT_SPAN = ATT_BLK * DILATIONS[-1]
ATT_RES_PER_STEP = 2
N_HEADS_B = 4
DV_B = 128
DK_B = 64
GATE_RANK = 16
GATE_TAU = 16.0
GLA_CHUNK = 64
N_GROUPS_C = 8
SGU_CHUNK = 128
D_FF = 2816
CONV_W = 3
A_WIDTH = N_HEADS_A * HEAD_DIM_A
B_QK = N_HEADS_B * DK_B
B_V = N_HEADS_B * DV_B
NEG = -1e30
EPS = 1e-6
LANES = 128
N_PAIRS_A = A_WIDTH // LANES
MXU_N = 256
FF_CHUNK = MXU_N
N_FF_CHUNKS = D_FF // FF_CHUNK

VMEM_LIMIT = 56 * 1024 * 1024


def _dot(a, b):
    return jnp.dot(a, b, preferred_element_type=F32)


def _dot_nt(a, b):
    return lax.dot_general(a, b, (((1,), (1,)), ((), ())), preferred_element_type=F32)


def _rms(x):
    return x * lax.rsqrt(jnp.mean(x * x, axis=-1, keepdims=True) + EPS)


def _const_spec(shape):
    nd = len(shape)
    return pl.BlockSpec(shape, lambda *_: (0,) * nd, pipeline_mode=pl.Buffered(1))


def _layer_spec(stacked, layer):
    nd = stacked.ndim - 1
    return pl.BlockSpec((None,) + stacked.shape[1:], lambda *_: (layer,) + (0,) * nd, pipeline_mode=pl.Buffered(1))


def _params(sem, vmem=VMEM_LIMIT):
    return pltpu.CompilerParams(dimension_semantics=sem, vmem_limit_bytes=vmem)


def _rope_tab_kernel(pos_ref, inv_ref, sgn_ref, c_ref, s_ref):
    ang = pos_ref[...].astype(F32) * inv_ref[...]
    c_ref[...] = jnp.cos(ang)
    s_ref[...] = sgn_ref[...] * jnp.sin(ang)


def _rope_tables(positions):
    t = positions.size
    half = ROT_DIM // 2
    inv = np.float64(ROPE_THETA) ** (-np.arange(half, dtype=np.float64) * (2.0 / ROT_DIM))
    dim = np.arange(LANES) % HEAD_DIM_A
    inv_lane = np.where(dim < ROT_DIM, inv[dim % half], 0.0).astype(np.float32)[None, :]
    sgn_lane = np.where(dim < half, -1.0, np.where(dim < ROT_DIM, 1.0, 0.0)).astype(np.float32)[None, :]
    tm = 1024
    pos = positions.reshape(t, 1)
    return pl.pallas_call(
        _rope_tab_kernel,
        grid=(t // tm,),
        in_specs=[pl.BlockSpec((tm, 1), lambda i: (i, 0)),
                  _const_spec((1, LANES)), _const_spec((1, LANES))],
        out_specs=[pl.BlockSpec((tm, LANES), lambda i: (i, 0))] * 2,
        out_shape=[jax.ShapeDtypeStruct((t, LANES), F32)] * 2,
        compiler_params=_params(("arbitrary",)),
        name="rope_tables",
    )(pos, jnp.asarray(inv_lane), jnp.asarray(sgn_lane))


def _proj_ab_kernel(h_ref, g_ref, w_ref, wgu_ref, bgu_ref, c_ref, s_ref, gb_ref, *refs, tiles_per_seq):
    nd = len(DILATIONS)
    qkv_refs = refs[:3 * nd]
    ob_ref, hn_ref, stage_ref, stage2_ref, qb_ref, kb_ref, vb_ref, rb_ref, la_ref, state_ref = refs[3 * nd:3 * nd + 10]
    gla_scratch = refs[3 * nd + 10:]
    assert nd <= 3
    tm = h_ref.shape[0]
    hn_ref[...] = (_rms(h_ref[...]) * g_ref[...]).astype(BF16)
    cos = c_ref[...]
    sin = s_ref[...]
    lane = lax.broadcasted_iota(jnp.int32, cos.shape, 1)
    first_half = (lane % HEAD_DIM_A) < (ROT_DIM // 2)

    def rope(x):
        partner = jnp.where(first_half,
                            pltpu.roll(x, LANES - ROT_DIM // 2, axis=1),
                            pltpu.roll(x, ROT_DIM // 2, axis=1))
        return x * cos + partner * sin

    def project_a(which):
        for jj in range(A_WIDTH // MXU_N):
            col = which * A_WIDTH + jj * MXU_N
            x2 = _dot(hn_ref[...], w_ref[:, col:col + MXU_N])
            for half in range(MXU_N // LANES):
                j = jj * (MXU_N // LANES) + half
                x = x2[:, half * LANES:(half + 1) * LANES]
                if which < 2:
                    x = rope(x)
                qkv_refs[which * nd][:, j * LANES:(j + 1) * LANES] = x.astype(BF16)
                stage_ref[which * N_PAIRS_A + j] = x

    @pl.when(pl.program_id(0) % tiles_per_seq == 0)
    def _():
        state_ref[...] = jnp.zeros_like(state_ref)

    b0 = 3 * A_WIDTH
    qb_ref[...] = _dot(hn_ref[...], w_ref[:, b0:b0 + B_QK])
    kb_ref[...] = _dot(hn_ref[...], w_ref[:, b0 + B_QK:b0 + 2 * B_QK])
    vb_ref[...] = _dot(hn_ref[...], w_ref[:, b0 + 2 * B_QK:b0 + 2 * B_QK + B_V]).astype(BF16)
    rb_ref[...] = _dot(hn_ref[...], w_ref[:, b0 + 2 * B_QK + B_V:b0 + 2 * B_QK + 2 * B_V])
    gl = _dot(hn_ref[...], w_ref[:, b0 + 2 * B_QK + 2 * B_V:])
    g = _dot(gl.astype(BF16), wgu_ref[...]) + bgu_ref[...]
    la_ref[...] = (jnp.minimum(g, 0.0) - jnp.log1p(jnp.exp(-jnp.abs(g)))) * (1.0 / GATE_TAU)
    n_scr = len(gla_scratch) // (N_HEADS_B // 2)
    pairs = []
    for p in range(N_HEADS_B // 2):
        qk = pl.ds(p * LANES, LANES)
        vr = pl.ds(p * 2 * DV_B, 2 * DV_B)
        pairs.append(_gla_phases(qb_ref.at[:, qk], kb_ref.at[:, qk], la_ref.at[:, qk], vb_ref.at[:, vr],
                                 rb_ref.at[:, vr], gb_ref, ob_ref.at[:, vr], state_ref.at[p],
                                 *gla_scratch[p * n_scr:(p + 1) * n_scr]))
    project_a(0)
    for operands, _, _, _, _ in pairs:
        operands()
    project_a(1)
    for _, scores, _, _, _ in pairs:
        scores()
    project_a(2)
    for _, _, intra, scan, _ in pairs:
        intra()
        scan()
    for _, _, _, _, outputs in pairs:
        outputs()
    for which in range(3):
        for j in range(N_PAIRS_A):
            u = which * N_PAIRS_A + j
            planes = [stage_ref.at[u]]
            for di in range(1, nd):
                ratio = DILATIONS[di] // DILATIONS[di - 1]
                n_rows = tm // DILATIONS[di]
                nxt = []
                for a in range(ratio):
                    for b, plane in enumerate(planes):
                        r = a * len(planes) + b
                        piece = plane[pl.ds(a, n_rows, stride=ratio), :]
                        qkv_refs[which * nd + di][r, :, j * LANES:(j + 1) * LANES] = piece.astype(BF16)
                        if di + 1 < nd:
                            plane_ref = stage2_ref.at[u, r]
                            plane_ref[...] = piece
                            nxt.append((r, plane_ref))
                planes = [p for _, p in sorted(nxt, key=lambda t: t[0])]


def _proj_ab(h, g, w_in, layer, wgu, bgu, cos, sin, g_out, seq, tm=512):
    t = h.shape[0]
    row = lambda n: pl.BlockSpec((tm, n), lambda i: (i, 0))
    specs = [row(A_WIDTH)] + [pl.BlockSpec((d, tm // d, A_WIDTH), lambda i: (0, i, 0)) for d in DILATIONS[1:]]
    shapes = [jax.ShapeDtypeStruct((t, A_WIDTH), BF16)]
    shapes += [jax.ShapeDtypeStruct((d, t // d, A_WIDTH), BF16) for d in DILATIONS[1:]]
    specs, shapes = specs * 3, shapes * 3
    n_pairs = N_HEADS_B // 2
    outs = pl.pallas_call(
        functools.partial(_proj_ab_kernel, tiles_per_seq=seq // tm),
        grid=(t // tm,),
        in_specs=[row(D_MODEL), _const_spec((1, D_MODEL)), _layer_spec(w_in, layer),
                  _const_spec(wgu.shape), _const_spec(bgu.shape), row(LANES), row(LANES), _const_spec((1, DV_B))],
        out_specs=specs + [row(B_V)],
        out_shape=shapes + [jax.ShapeDtypeStruct((t, B_V), BF16)],
        scratch_shapes=[pltpu.VMEM((tm, D_MODEL), BF16), pltpu.VMEM((3 * N_PAIRS_A, tm, LANES), F32),
                        pltpu.VMEM((3 * N_PAIRS_A, DILATIONS[1], tm // DILATIONS[1], LANES), F32),
                        pltpu.VMEM((tm, B_QK), F32), pltpu.VMEM((tm, B_QK), F32), pltpu.VMEM((tm, B_V), BF16),
                        pltpu.VMEM((tm, B_V), F32), pltpu.VMEM((tm, B_QK), F32),
                        pltpu.VMEM((n_pairs, 2 * DV_B, LANES), F32)] + _gla_scratch(tm) * n_pairs,
        compiler_params=_params(("arbitrary",)),
        name="proj_ab",
    )(h, g, w_in, wgu, bgu, cos, sin, g_out)
    return outs[:3 * len(DILATIONS)], outs[3 * len(DILATIONS)]


def _attn_bias():
    a = np.arange(ATT_BLK)[None, :]
    c = np.arange(2 * ATT_BLK)[:, None]
    band = (c >= a) & (c <= a + ATT_BLK)
    first = band & (c >= ATT_BLK)
    return np.where(np.stack([band, first]), 0.0, NEG).astype(np.float32)


def _attn_kernel(*refs):
    nb = len(DILATIONS)
    in_refs = refs[:3 * nb]
    bias_ref = refs[3 * nb]
    out_ref = refs[3 * nb + 1]
    scratch = refs[3 * nb + 2:]
    acc_refs, lse_refs = scratch[:nb], scratch[nb:2 * nb]
    s_ref, p_ref, inv_ref, stat_ref = scratch[2 * nb:2 * nb + 4]
    kring_refs, vring_refs = scratch[2 * nb + 4:3 * nb + 4], scratch[3 * nb + 4:]
    n = pl.program_id(1)
    n_res = DILATIONS[-1]
    for sub in range(ATT_RES_PER_STEP):
        _attn_substep(n, pl.program_id(2) * ATT_RES_PER_STEP + sub, sub, in_refs, bias_ref, acc_refs, lse_refs,
                      s_ref, p_ref, inv_ref, stat_ref, kring_refs, vring_refs)

    @pl.when(pl.program_id(2) == n_res // ATT_RES_PER_STEP - 1)
    def _():
        chunk = 256
        for hp in range(A_WIDTH // LANES):
            def body(i, carry):
                rows = pl.ds(pl.multiple_of(i * chunk, chunk), chunk)
                lse = [ref[hp, rows, :] for ref in lse_refs]
                top = functools.reduce(jnp.maximum, lse)
                w = [jnp.exp(x - top) for x in lse]
                num = functools.reduce(jnp.add, [wi * ref[hp, rows, :] for wi, ref in zip(w, acc_refs)])
                out_ref[0, rows, hp * LANES:(hp + 1) * LANES] = (num / functools.reduce(jnp.add, w)).astype(BF16)
                return carry
            lax.fori_loop(0, ATT_SPAN // chunk, body, 0)


def _attn_substep(n, r, sub, in_refs, bias_ref, acc_refs, lse_refs, s_ref, p_ref, inv_ref, stat_ref,
                  kring_refs, vring_refs):
    nb = len(DILATIONS)
    n_res = DILATIONS[-1]

    def tile(ref, d):
        return ref.at[0, 0, pl.ds(sub * ATT_BLK, ATT_BLK)] if d == 1 else ref.at[sub, 0]

    lane = lax.broadcasted_iota(jnp.int32, (ATT_BLK, LANES), 1)
    low_head = lane < HEAD_DIM_A
    zero = jnp.zeros((), BF16)
    scale = jnp.asarray(HEAD_DIM_A ** -0.5, BF16)
    eye = jnp.where(lax.broadcasted_iota(jnp.int32, (ATT_BLK, ATT_BLK), 0)
                    == lax.broadcasted_iota(jnp.int32, (ATT_BLK, ATT_BLK), 1), 1.0, 0.0).astype(BF16)
    slots = [r % d for d in DILATIONS]
    blks = [n * (n_res // d) + r // d for d in DILATIONS]

    @pl.when(n == 0)
    def _():
        for g in range(nb):
            @pl.when(blks[g] == 0)
            def _():
                kring_refs[g][slots[g]] = jnp.zeros((ATT_BLK, A_WIDTH), BF16)
                vring_refs[g][slots[g]] = jnp.zeros((ATT_BLK, A_WIDTH), BF16)

    for g, d in enumerate(DILATIONS):
        q_ref, k_ref = tile(in_refs[3 * g], d), tile(in_refs[3 * g + 1], d)
        bias_t = bias_ref[jnp.where(blks[g] == 0, 1, 0)]
        for hp in range(N_PAIRS_A):
            sl = slice(hp * LANES, (hp + 1) * LANES)
            q2 = q_ref[:, sl] * scale
            kcat = jnp.concatenate([kring_refs[g][slots[g], :, sl], k_ref[:, sl]], axis=0)
            rhs = jnp.concatenate([kcat, bias_t], axis=1)
            for hh in range(2):
                qm = jnp.where(low_head == (hh == 0), q2, zero)
                s_ref[(g * N_PAIRS_A + hp) * 2 + hh] = _dot_nt(jnp.concatenate([qm, eye], axis=1), rhs)

    for u in range(nb * N_PAIRS_A * 2):
        s = s_ref[u]
        m = jnp.max(s, axis=-1, keepdims=True)
        p = jnp.exp(s - m)
        l = jnp.sum(p, axis=-1, keepdims=True)
        p_ref[u] = p.astype(BF16)
        inv_ref[u] = jnp.broadcast_to(1.0 / l, (ATT_BLK, LANES))
        stat_ref[u] = jnp.broadcast_to(m + jnp.log(l), (ATT_BLK, LANES))

    for g, d in enumerate(DILATIONS):
        v_ref = tile(in_refs[3 * g + 2], d)
        if d == 1:
            rows = pl.ds(pl.multiple_of(r * ATT_BLK, ATT_BLK), ATT_BLK)
        else:
            rows = pl.ds((r // d) * (ATT_BLK * d) + r % d, ATT_BLK, stride=d)
        for hp in range(N_PAIRS_A):
            sl = slice(hp * LANES, (hp + 1) * LANES)
            vcat = jnp.concatenate([vring_refs[g][slots[g], :, sl], v_ref[:, sl]], axis=0)
            u0 = (g * N_PAIRS_A + hp) * 2
            o0 = _dot(p_ref[u0], vcat) * inv_ref[u0]
            o1 = _dot(p_ref[u0 + 1], vcat) * inv_ref[u0 + 1]
            acc_refs[g][hp, rows, :] = jnp.where(low_head, o0, o1)
            lse_refs[g][hp, rows, :] = jnp.where(low_head, stat_ref[u0], stat_ref[u0 + 1])
        kring_refs[g][slots[g]] = tile(in_refs[3 * g + 1], d)[...]
        vring_refs[g][slots[g]] = v_ref[...]


def _attention(qkv, batch, seq):
    n_res = DILATIONS[-1]
    n_span = seq // ATT_SPAN
    nd = len(DILATIONS)
    operands, in_specs = [], []
    res = ATT_RES_PER_STEP
    for di, d in enumerate(DILATIONS):
        per = n_res // d
        assert d == 1 or d % res == 0
        if d == 1:
            shape = (1, 1, res * ATT_BLK, A_WIDTH)
            cur = lambda b, n, rs, per=per: (0, b, (n * per) // res + rs, 0)
        else:
            shape = (res, 1, ATT_BLK, A_WIDTH)
            cur = lambda b, n, rs, d=d, per=per: (rs % (d // res), b, n * per + (rs * res) // d, 0)
        operands += [qkv[which * nd + di].reshape(d, batch, seq // d, A_WIDTH) for which in range(3)]
        in_specs += [pl.BlockSpec(shape, cur)] * 3
    bias = jnp.asarray(_attn_bias(), BF16)
    n_units = len(DILATIONS) * N_HEADS_A
    rings = [pltpu.VMEM((d, ATT_BLK, A_WIDTH), BF16) for d in DILATIONS]
    out = pl.pallas_call(
        _attn_kernel,
        grid=(batch, n_span, n_res // res),
        in_specs=in_specs + [_const_spec(bias.shape)],
        out_specs=pl.BlockSpec((1, ATT_SPAN, A_WIDTH), lambda b, n, rs: (b, n, 0)),
        out_shape=jax.ShapeDtypeStruct((batch, seq, A_WIDTH), BF16),
        scratch_shapes=[pltpu.VMEM((N_PAIRS_A, ATT_SPAN, LANES), F32)] * (2 * len(DILATIONS))
        + [pltpu.VMEM((n_units, ATT_BLK, 2 * ATT_BLK), F32), pltpu.VMEM((n_units, ATT_BLK, 2 * ATT_BLK), BF16),
           pltpu.VMEM((n_units, ATT_BLK, LANES), F32), pltpu.VMEM((n_units, ATT_BLK, LANES), F32)]
        + rings + rings,
        compiler_params=_params(("arbitrary", "arbitrary", "arbitrary")),
        name="dilated_attention",
    )(*operands, bias)
    return out.reshape(batch * seq, A_WIDTH)


def _gla_scratch(tm):
    n_chunks = tm // GLA_CHUNK
    return ([pltpu.VMEM((tm, LANES), BF16)] * 4
            + [pltpu.VMEM((n_chunks, LANES), F32), pltpu.VMEM((n_chunks, 2 * DV_B, LANES), F32),
               pltpu.VMEM((n_chunks, 2 * DV_B, LANES), BF16), pltpu.VMEM((n_chunks, GLA_CHUNK, 2 * DV_B), F32),
               pltpu.VMEM((n_chunks, GLA_CHUNK, 2 * GLA_CHUNK), BF16)])


def _gla_phases(q_ref, k_ref, la_ref, v_ref, r_ref, g_ref, o_ref, state_ref,
                qi_ref, ki_ref, kd_ref, qg_ref, dec_ref, kv_ref, sb_ref, oi_ref, att_ref):
    n_chunks = q_ref.shape[0] // GLA_CHUNK
    c = GLA_CHUNK
    two = 2 * c

    def operands():
        row2 = lax.broadcasted_iota(jnp.int32, (two, two), 0)
        col2 = lax.broadcasted_iota(jnp.int32, (two, two), 1)
        tril2 = jnp.where((col2 <= row2) & (col2 // c == row2 // c), 1.0, 0.0).astype(BF16)
        for gi in range(n_chunks // 2):
            _gla_operands(gi, tril2, q_ref, k_ref, la_ref, qi_ref, ki_ref, kd_ref, qg_ref, dec_ref)

    lane_k = lax.broadcasted_iota(jnp.int32, (c, LANES), 1)
    lane_v = lax.broadcasted_iota(jnp.int32, (c, 2 * DV_B), 1)
    zero = jnp.zeros((), BF16)

    def scores():
        att_row = lax.broadcasted_iota(jnp.int32, (c, 2 * c), 0)
        att_col = lax.broadcasted_iota(jnp.int32, (c, 2 * c), 1)
        causal = (att_col % c) <= att_row
        inc_row = lax.broadcasted_iota(jnp.int32, (2 * DV_B, LANES), 0)
        inc_col = lax.broadcasted_iota(jnp.int32, (2 * DV_B, LANES), 1)
        own_block = (inc_row // DV_B) == (inc_col // DK_B)
        for ci in range(n_chunks):
            rows = slice(ci * c, (ci + 1) * c)
            ki = ki_ref[rows, :]
            k_stack = jnp.concatenate([jnp.where(lane_k < DK_B, ki, zero), jnp.where(lane_k >= DK_B, ki, zero)],
                                      axis=0)
            att = jnp.where(causal, _dot_nt(qi_ref[rows, :], k_stack), 0.0)
            att_ref[ci] = att.astype(BF16)
            v_t = v_ref[rows, :].astype(F32).T.astype(BF16)
            kv_ref[ci] = jnp.where(own_block, _dot(v_t, kd_ref[rows, :]), 0.0)

    def intra():
        for ci in range(n_chunks):
            v = v_ref[ci * c:(ci + 1) * c, :]
            v_blocks = jnp.concatenate([jnp.where(lane_v < DV_B, v, zero), jnp.where(lane_v >= DV_B, v, zero)],
                                       axis=0)
            oi_ref[ci] = _dot(att_ref[ci], v_blocks)

    def scan():
        state = state_ref[...]
        for ci in range(n_chunks):
            sb_ref[ci] = state.astype(BF16)
            state = state * dec_ref[ci:ci + 1, :] + kv_ref[ci]
        state_ref[...] = state

    def outputs():
        gain = g_ref[...]
        for ci in range(n_chunks):
            rows = slice(ci * c, (ci + 1) * c)
            o = oi_ref[ci] + _dot_nt(qg_ref[rows, :], sb_ref[ci])
            rg = r_ref[rows, :]
            gate = rg * jax.nn.sigmoid(rg)
            for h in range(2):
                cols = slice(h * DV_B, (h + 1) * DV_B)
                o_ref[rows, cols] = (_rms(o[:, cols]) * gain * gate[:, cols]).astype(BF16)

    return operands, scores, intra, scan, outputs


def _gla_operands(gi, tril2, q_ref, k_ref, la_ref, qi_ref, ki_ref, kd_ref, qg_ref, dec_ref):
    c = GLA_CHUNK
    two = 2 * c
    rows = slice(gi * two, (gi + 1) * two)
    la = la_ref[rows, :]
    hi = la.astype(BF16)
    rem = la - hi.astype(F32)
    mid = rem.astype(BF16)
    lo = (rem - mid.astype(F32)).astype(BF16)
    b3 = _dot(tril2, jnp.concatenate([hi, mid, lo], axis=1))
    b = b3[:, :LANES] + b3[:, LANES:2 * LANES] + b3[:, 2 * LANES:]
    b_last = jnp.concatenate([jnp.broadcast_to(b[c - 1:c], (c, LANES)),
                              jnp.broadcast_to(b[two - 1:two], (c, LANES))], axis=0)
    b_mid = jnp.concatenate([jnp.broadcast_to(b[c // 2 - 1:c // 2], (c, LANES)),
                             jnp.broadcast_to(b[c + c // 2 - 1:c + c // 2], (c, LANES))], axis=0)
    q = q_ref[rows, :] * (DK_B ** -0.5)
    k = k_ref[rows, :]
    qi_ref[rows, :] = (q * jnp.exp(b - b_mid)).astype(BF16)
    ki_ref[rows, :] = (k * jnp.exp(b_mid - b)).astype(BF16)
    kd_ref[rows, :] = (k * jnp.exp(b_last - b)).astype(BF16)
    qg_ref[rows, :] = (q * jnp.exp(b)).astype(BF16)
    dec_ref[2 * gi:2 * gi + 1, :] = jnp.exp(b[c - 1:c])
    dec_ref[2 * gi + 1:2 * gi + 2, :] = jnp.exp(b[two - 1:two])


def _sgu_kernel(h_ref, g_ref, w_in_ref, g_sgu_ref, ws_ref, bias_ref, o1_ref, o2_ref, hn_ref, v_ref, u_ref, *,
                n_chunks):
    d = D_MODEL
    hn_ref[...] = (_rms(h_ref[...]) * g_ref[...]).astype(BF16)
    v = jax.nn.gelu(_dot(hn_ref[...], w_in_ref[:, d:]))
    for cc in range(d // MXU_N):
        cols = slice(cc * MXU_N, (cc + 1) * MXU_N)
        u_ref[:, cols] = jax.nn.gelu(_dot(hn_ref[...], w_in_ref[:, cols]))
    v_ref[...] = (_rms(v) * g_sgu_ref[...]).astype(BF16)
    row = lax.broadcasted_iota(jnp.int32, (SGU_CHUNK, SGU_CHUNK), 0)
    col = lax.broadcasted_iota(jnp.int32, (SGU_CHUNK, SGU_CHUNK), 1)
    tril = col <= row
    half = d // 2
    per_dot = MXU_N // SGU_CHUNK
    for g in range(N_GROUPS_C):
        ws = jnp.where(tril, ws_ref[g], 0.0).astype(BF16)
        cols = slice(g * SGU_CHUNK, (g + 1) * SGU_CHUNK)
        for cc in range(n_chunks // per_dot):
            chunks = range(cc * per_dot, (cc + 1) * per_dot)
            v_cat = jnp.concatenate([v_ref[ci * SGU_CHUNK:(ci + 1) * SGU_CHUNK, cols] for ci in chunks], axis=1)
            sv_cat = _dot(ws, v_cat)
            for k, ci in enumerate(chunks):
                rows = slice(ci * SGU_CHUNK, (ci + 1) * SGU_CHUNK)
                sv = sv_cat[:, k * SGU_CHUNK:(k + 1) * SGU_CHUNK] + bias_ref[:, cols]
                y = (u_ref[rows, cols] * sv).astype(BF16)
                if g < N_GROUPS_C // 2:
                    o1_ref[rows, cols] = y
                else:
                    o2_ref[rows, g * SGU_CHUNK - half:(g + 1) * SGU_CHUNK - half] = y


def _sgu(h, g, w_in, g_sgu, ws, layer, bias, tm=512):
    t = h.shape[0]
    row = lambda n: pl.BlockSpec((tm, n), lambda i: (i, 0))
    half = D_MODEL // 2
    return pl.pallas_call(
        functools.partial(_sgu_kernel, n_chunks=tm // SGU_CHUNK),
        grid=(t // tm,),
        in_specs=[row(D_MODEL), _const_spec((1, D_MODEL)), _layer_spec(w_in, layer),
                  _const_spec((1, D_MODEL)), _layer_spec(ws, layer), _const_spec(bias.shape)],
        out_specs=[row(half), row(half)],
        out_shape=[jax.ShapeDtypeStruct((t, half), BF16)] * 2,
        scratch_shapes=[pltpu.VMEM((tm, D_MODEL), BF16), pltpu.VMEM((tm, D_MODEL), BF16),
                        pltpu.VMEM((tm, D_MODEL), F32)],
        compiler_params=_params(("arbitrary",)),
        name="sgu",
    )(h, g, w_in, g_sgu, ws, bias)


def _shift_rows(z, prev, k):
    rolled = pltpu.roll(z, k, axis=0)
    head = jnp.where(lax.broadcasted_iota(jnp.int32, prev.shape, 0) < k,
                     pltpu.roll(prev, k, axis=0), rolled[:8])
    return jnp.concatenate([head, rolled[8:]], axis=0)


def _ffn_kernel(h_ref, o1_ref, o2_ref, w_o_ref, g_ref, w_up_ref, cw_ref, cb_ref, w_down_ref, gf_ref,
                out_ref, hn_ref, act_ref, carry_ref, *, final_norm):
    @pl.when(pl.program_id(1) == 0)
    def _():
        carry_ref[...] = jnp.zeros_like(carry_ref)

    tm = h_ref.shape[0]
    o = jnp.concatenate([o1_ref[...], o2_ref[...]], axis=1)
    h1 = h_ref[...] + _dot(o, w_o_ref[...])
    out_ref[...] = h1
    hn_ref[...] = (_rms(h1) * g_ref[...]).astype(BF16)

    def conv(c, part):
        off = part * D_FF + c * FF_CHUNK
        idx = part * N_FF_CHUNKS + c
        z = _dot(hn_ref[...], w_up_ref[:, off:off + FF_CHUNK])
        prev = carry_ref[idx]
        carry_ref[idx] = z[tm - 8:]
        w = cw_ref[:, off:off + FF_CHUNK]
        acc = _shift_rows(z, prev, 2) * w[0:1] + _shift_rows(z, prev, 1) * w[1:2] + z * w[2:3]
        return acc + cb_ref[:, off:off + FF_CHUNK]

    for c in range(N_FF_CHUNKS):
        gate = conv(c, 0)
        up = conv(c, 1)
        act_ref[:, c * FF_CHUNK:(c + 1) * FF_CHUNK] = (gate * jax.nn.sigmoid(gate) * up).astype(BF16)
    h2 = out_ref[...] + _dot(act_ref[...], w_down_ref[...])
    if final_norm:
        h2 = _rms(h2) * gf_ref[...]
    out_ref[...] = h2


def _ffn(h, o1, o2, w_o, mixer_layer, g, w_up, conv_w, conv_b, w_down, layer, g_final, batch, seq, final_norm,
         tm=1024):
    t = batch * seq
    per = seq // tm
    row = lambda n: pl.BlockSpec((tm, n), lambda b, i: (b * per + i, 0))
    return pl.pallas_call(
        functools.partial(_ffn_kernel, final_norm=final_norm),
        grid=(batch, per),
        in_specs=[row(D_MODEL), row(D_MODEL // 2), row(D_MODEL // 2), _layer_spec(w_o, mixer_layer),
                  _const_spec((1, D_MODEL)), _layer_spec(w_up, layer), _layer_spec(conv_w, layer),
                  _const_spec(conv_b.shape), _layer_spec(w_down, layer), _const_spec((1, D_MODEL))],
        out_specs=row(D_MODEL),
        out_shape=jax.ShapeDtypeStruct((t, D_MODEL), F32),
        scratch_shapes=[pltpu.VMEM((tm, D_MODEL), BF16), pltpu.VMEM((tm, D_FF), BF16),
                        pltpu.VMEM((2 * N_FF_CHUNKS, 8, FF_CHUNK), F32)],
        compiler_params=_params(("arbitrary", "arbitrary")),
        name="ffn",
    )(h, o1, o2, w_o, g, w_up, conv_w, conv_b, w_down, g_final)


def kernel(x, positions, norm_mix, norm_ffn, w_in_ab, w_gate_up, b_gate_up, g_out_b, w_out_ab, w_in_c,
           g_sgu, w_spatial, b_spatial, w_out_c, w_up, conv_w, conv_b, w_down, norm_final):
    batch, seq, d = x.shape
    depth = norm_mix.shape[0]
    t = batch * seq
    h = x.reshape(t, d)
    cos, sin = _rope_tables(positions)
    row = lambda v: v.reshape(1, -1)
    w_in_ab_b = jnp.pad(w_in_ab, ((0, 0), (0, 0), (0, LANES - GATE_RANK))).astype(BF16)
    w_in_c_b = w_in_c.astype(BF16)
    w_out_b = (w_out_ab.astype(BF16), w_out_c.astype(BF16))
    w_up_b, w_down_b = w_up.astype(BF16), w_down.astype(BF16)
    for layer in range(depth):
        i = layer // 2
        if layer % 2 == 0:
            wgu = jnp.pad(w_gate_up[i], ((0, LANES - GATE_RANK), (0, 0))).astype(BF16)
            qkv, o2 = _proj_ab(h, row(norm_mix[layer]), w_in_ab_b, i, wgu, row(b_gate_up[i]), cos, sin,
                               row(g_out_b[i]), seq)
            o1 = _attention(qkv, batch, seq)
        else:
            bias = jnp.repeat(b_spatial[i].T, SGU_CHUNK, axis=1)
            o1, o2 = _sgu(h, row(norm_mix[layer]), w_in_c_b, row(g_sgu[i]), w_spatial, i, bias)
        h = _ffn(h, o1, o2, w_out_b[layer % 2], i, row(norm_ffn[layer]), w_up_b, conv_w, row(conv_b[layer]),
                 w_down_b, layer, row(norm_final), batch, seq, final_norm=(layer == depth - 1))
    return h.reshape(batch, seq, d)
```

```python
import functools

import numpy as np
import jax
import jax.numpy as jnp
from jax import lax
from jax.experimental import pallas as pl
from jax.experimental.pallas import tpu as pltpu

F32 = jnp.float32
BF16 = jnp.bfloat16

D_MODEL = 1024
N_HEADS_A = 8
HEAD_DIM_A = 64
ROT_DIM = HEAD_DIM_A // 4
ROPE_THETA = 500000.0
DILATIONS = (1, 4, 16)
ATT_BLK = 128
ATT_SPAN = ATT_BLK * DILATIONS[-1]
ATT_RES_PER_STEP = 2
N_HEADS_B = 4
DV_B = 128
DK_B = 64
GATE_RANK = 16
GATE_TAU = 16.0
GLA_CHUNK = 64
N_GROUPS_C = 8
SGU_CHUNK = 128
D_FF = 2816
CONV_W = 3
A_WIDTH = N_HEADS_A * HEAD_DIM_A
B_QK = N_HEADS_B * DK_B
B_V = N_HEADS_B * DV_B
NEG = -1e30
EPS = 1e-6
LANES = 128
N_PAIRS_A = A_WIDTH // LANES
MXU_N = 256
FF_CHUNK = MXU_N
N_FF_CHUNKS = D_FF // FF_CHUNK

VMEM_LIMIT = 56 * 1024 * 1024


def _dot(a, b):
    return jnp.dot(a, b, preferred_element_type=F32)


def _dot_nt(a, b):
    return lax.dot_general(a, b, (((1,), (1,)), ((), ())), preferred_element_type=F32)


def _rms(x):
    return x * lax.rsqrt(jnp.mean(x * x, axis=-1, keepdims=True) + EPS)


def _const_spec(shape):
    nd = len(shape)
    return pl.BlockSpec(shape, lambda *_: (0,) * nd, pipeline_mode=pl.Buffered(1))


def _layer_spec(stacked, layer):
    nd = stacked.ndim - 1
    return pl.BlockSpec((None,) + stacked.shape[1:], lambda *_: (layer,) + (0,) * nd, pipeline_mode=pl.Buffered(1))


def _params(sem, vmem=VMEM_LIMIT):
    return pltpu.CompilerParams(dimension_semantics=sem, vmem_limit_bytes=vmem)


def _rope_tab_kernel(pos_ref, inv_ref, sgn_ref, c_ref, s_ref):
    ang = pos_ref[...].astype(F32) * inv_ref[...]
    c_ref[...] = jnp.cos(ang)
    s_ref[...] = sgn_ref[...] * jnp.sin(ang)


def _rope_tables(positions):
    t = positions.size
    half = ROT_DIM // 2
    inv = np.float64(ROPE_THETA) ** (-np.arange(half, dtype=np.float64) * (2.0 / ROT_DIM))
    dim = np.arange(LANES) % HEAD_DIM_A
    inv_lane = np.where(dim < ROT_DIM, inv[dim % half], 0.0).astype(np.float32)[None, :]
    sgn_lane = np.where(dim < half, -1.0, np.where(dim < ROT_DIM, 1.0, 0.0)).astype(np.float32)[None, :]
    tm = 1024
    pos = positions.reshape(t, 1)
    return pl.pallas_call(
        _rope_tab_kernel,
        grid=(t // tm,),
        in_specs=[pl.BlockSpec((tm, 1), lambda i: (i, 0)),
                  _const_spec((1, LANES)), _const_spec((1, LANES))],
        out_specs=[pl.BlockSpec((tm, LANES), lambda i: (i, 0))] * 2,
        out_shape=[jax.ShapeDtypeStruct((t, LANES), F32)] * 2,
        compiler_params=_params(("arbitrary",)),
        name="rope_tables",
    )(pos, jnp.asarray(inv_lane), jnp.asarray(sgn_lane))


def _proj_ab_kernel(h_ref, g_ref, w_ref, wg_ref, wgu_ref, bgu_ref, c_ref, s_ref, gb_ref, *refs, tiles_per_seq):
    nd = len(DILATIONS)
    qkv_refs = refs[:3 * nd]
    ob_ref, hn_ref, stage_ref, stage2_ref, qb_ref, kb_ref, vb_ref, rb_ref, la_ref, state_ref = refs[3 * nd:3 * nd + 10]
    gla_scratch = refs[3 * nd + 10:]
    assert nd <= 3
    tm = h_ref.shape[0]
    hn_ref[...] = (_rms(h_ref[...]) * g_ref[...]).astype(BF16)
    cos = c_ref[...]
    sin = s_ref[...]
    lane = lax.broadcasted_iota(jnp.int32, cos.shape, 1)
    first_half = (lane % HEAD_DIM_A) < (ROT_DIM // 2)

    def rope(x):
        partner = jnp.where(first_half,
                            pltpu.roll(x, LANES - ROT_DIM // 2, axis=1),
                            pltpu.roll(x, ROT_DIM // 2, axis=1))
        return x * cos + partner * sin

    def project_a(which):
        for jj in range(A_WIDTH // MXU_N):
            col = which * A_WIDTH + jj * MXU_N
            x2 = _dot(hn_ref[...], w_ref[:, col:col + MXU_N])
            for half in range(MXU_N // LANES):
                j = jj * (MXU_N // LANES) + half
                x = x2[:, half * LANES:(half + 1) * LANES]
                if which < 2:
                    x = rope(x)
                qkv_refs[which * nd][:, j * LANES:(j + 1) * LANES] = x.astype(BF16)
                stage_ref[which * N_PAIRS_A + j] = x

    @pl.when(pl.program_id(0) % tiles_per_seq == 0)
    def _():
        state_ref[...] = jnp.zeros_like(state_ref)

    b0 = 3 * A_WIDTH
    qb_ref[...] = _dot(hn_ref[...], w_ref[:, b0:b0 + B_QK])
    kb_ref[...] = _dot(hn_ref[...], w_ref[:, b0 + B_QK:b0 + 2 * B_QK])
    vb_ref[...] = _dot(hn_ref[...], w_ref[:, b0 + 2 * B_QK:b0 + 2 * B_QK + B_V]).astype(BF16)
    rb_ref[...] = _dot(hn_ref[...], w_ref[:, b0 + 2 * B_QK + B_V:b0 + 2 * B_QK + 2 * B_V])
    gl = _dot(hn_ref[...], wg_ref[...])
    g = _dot(gl.astype(BF16), wgu_ref[...]) + bgu_ref[...]
    la_ref[...] = (jnp.minimum(g, 0.0) - jnp.log1p(jnp.exp(-jnp.abs(g)))) * (1.0 / GATE_TAU)
    n_scr = len(gla_scratch) // (N_HEADS_B // 2)
    pairs = []
    for p in range(N_HEADS_B // 2):
        qk = pl.ds(p * LANES, LANES)
        vr = pl.ds(p * 2 * DV_B, 2 * DV_B)
        pairs.append(_gla_phases(qb_ref.at[:, qk], kb_ref.at[:, qk], la_ref.at[:, qk], vb_ref.at[:, vr],
                                 rb_ref.at[:, vr], gb_ref, ob_ref.at[:, vr], state_ref.at[p],
                                 *gla_scratch[p * n_scr:(p + 1) * n_scr]))
    project_a(0)
    for operands, _, _, _, _ in pairs:
        operands()
    project_a(1)
    for _, scores, _, _, _ in pairs:
        scores()
    project_a(2)
    for _, _, intra, scan, _ in pairs:
        intra()
        scan()
    for _, _, _, _, outputs in pairs:
        outputs()
    for which in range(3):
        for j in range(N_PAIRS_A):
            u = which * N_PAIRS_A + j
            planes = [stage_ref.at[u]]
            for di in range(1, nd):
                ratio = DILATIONS[di] // DILATIONS[di - 1]
                n_rows = tm // DILATIONS[di]
                nxt = []
                for a in range(ratio):
                    for b, plane in enumerate(planes):
                        r = a * len(planes) + b
                        piece = plane[pl.ds(a, n_rows, stride=ratio), :]
                        qkv_refs[which * nd + di][r, :, j * LANES:(j + 1) * LANES] = piece.astype(BF16)
                        if di + 1 < nd:
                            plane_ref = stage2_ref.at[u, r]
                            plane_ref[...] = piece
                            nxt.append((r, plane_ref))
                planes = [p for _, p in sorted(nxt, key=lambda t: t[0])]


def _proj_ab(h, g, w_in, wg, layer, wgu, bgu, cos, sin, g_out, seq, tm=512):
    t = h.shape[0]
    row = lambda n: pl.BlockSpec((tm, n), lambda i: (i, 0))
    specs = [row(A_WIDTH)] + [pl.BlockSpec((d, tm // d, A_WIDTH), lambda i: (0, i, 0)) for d in DILATIONS[1:]]
    shapes = [jax.ShapeDtypeStruct((t, A_WIDTH), BF16)]
    shapes += [jax.ShapeDtypeStruct((d, t // d, A_WIDTH), BF16) for d in DILATIONS[1:]]
    specs, shapes = specs * 3, shapes * 3
    n_pairs = N_HEADS_B // 2
    outs = pl.pallas_call(
        functools.partial(_proj_ab_kernel, tiles_per_seq=seq // tm),
        grid=(t // tm,),
        in_specs=[row(D_MODEL), _const_spec((1, D_MODEL)), _layer_spec(w_in, layer), _layer_spec(wg, layer),
                  _layer_spec(wgu, layer), _const_spec(bgu.shape), row(LANES), row(LANES), _const_spec((1, DV_B))],
        out_specs=specs + [row(B_V)],
        out_shape=shapes + [jax.ShapeDtypeStruct((t, B_V), BF16)],
        scratch_shapes=[pltpu.VMEM((tm, D_MODEL), BF16), pltpu.VMEM((3 * N_PAIRS_A, tm, LANES), F32),
                        pltpu.VMEM((3 * N_PAIRS_A, DILATIONS[1], tm // DILATIONS[1], LANES), F32),
                        pltpu.VMEM((tm, B_QK), F32), pltpu.VMEM((tm, B_QK), F32), pltpu.VMEM((tm, B_V), BF16),
                        pltpu.VMEM((tm, B_V), F32), pltpu.VMEM((tm, B_QK), F32),
                        pltpu.VMEM((n_pairs, 2 * DV_B, LANES), F32)] + _gla_scratch(tm) * n_pairs,
        compiler_params=_params(("arbitrary",)),
        name="proj_ab",
    )(h, g, w_in, wg, wgu, bgu, cos, sin, g_out)
    return outs[:3 * len(DILATIONS)], outs[3 * len(DILATIONS)]


def _attn_bias():
    a = np.arange(ATT_BLK)[None, :]
    c = np.arange(2 * ATT_BLK)[:, None]
    band = (c >= a) & (c <= a + ATT_BLK)
    first = band & (c >= ATT_BLK)
    return np.where(np.stack([band, first]), 0.0, NEG).astype(np.float32)


def _attn_kernel(*refs):
    nb = len(DILATIONS)
    in_refs = refs[:3 * nb]
    bias_ref = refs[3 * nb]
    out_ref = refs[3 * nb + 1]
    scratch = refs[3 * nb + 2:]
    acc_refs, lse_refs = scratch[:nb], scratch[nb:2 * nb]
    s_ref, p_ref, inv_ref, stat_ref = scratch[2 * nb:2 * nb + 4]
    kring_refs, vring_refs = scratch[2 * nb + 4:3 * nb + 4], scratch[3 * nb + 4:]
    n = pl.program_id(1)
    n_res = DILATIONS[-1]
    for sub in range(ATT_RES_PER_STEP):
        _attn_substep(n, pl.program_id(2) * ATT_RES_PER_STEP + sub, sub, in_refs, bias_ref, acc_refs, lse_refs,
                      s_ref, p_ref, inv_ref, stat_ref, kring_refs, vring_refs)

    @pl.when(pl.program_id(2) == n_res // ATT_RES_PER_STEP - 1)
    def _():
        chunk = 256
        for hp in range(A_WIDTH // LANES):
            def body(i, carry):
                rows = pl.ds(pl.multiple_of(i * chunk, chunk), chunk)
                lse = [ref[hp, rows, :] for ref in lse_refs]
                top = functools.reduce(jnp.maximum, lse)
                w = [jnp.exp(x - top) for x in lse]
                num = functools.reduce(jnp.add, [wi * ref[hp, rows, :] for wi, ref in zip(w, acc_refs)])
                out_ref[0, rows, hp * LANES:(hp + 1) * LANES] = (num / functools.reduce(jnp.add, w)).astype(BF16)
                return carry
            lax.fori_loop(0, ATT_SPAN // chunk, body, 0)


def _attn_substep(n, r, sub, in_refs, bias_ref, acc_refs, lse_refs, s_ref, p_ref, inv_ref, stat_ref,
                  kring_refs, vring_refs):
    nb = len(DILATIONS)
    n_res = DILATIONS[-1]

    def tile(ref, d):
        return ref.at[0, 0, pl.ds(sub * ATT_BLK, ATT_BLK)] if d == 1 else ref.at[sub, 0]

    lane = lax.broadcasted_iota(jnp.int32, (ATT_BLK, LANES), 1)
    low_head = lane < HEAD_DIM_A
    zero = jnp.zeros((), BF16)
    scale = jnp.asarray(HEAD_DIM_A ** -0.5, BF16)
    eye = jnp.where(lax.broadcasted_iota(jnp.int32, (ATT_BLK, ATT_BLK), 0)
                    == lax.broadcasted_iota(jnp.int32, (ATT_BLK, ATT_BLK), 1), 1.0, 0.0).astype(BF16)
    slots = [r % d for d in DILATIONS]
    blks = [n * (n_res // d) + r // d for d in DILATIONS]

    @pl.when(n == 0)
    def _():
        for g in range(nb):
            @pl.when(blks[g] == 0)
            def _():
                kring_refs[g][slots[g]] = jnp.zeros((ATT_BLK, A_WIDTH), BF16)
                vring_refs[g][slots[g]] = jnp.zeros((ATT_BLK, A_WIDTH), BF16)

    for g, d in enumerate(DILATIONS):
        q_ref, k_ref = tile(in_refs[3 * g], d), tile(in_refs[3 * g + 1], d)
        bias_t = bias_ref[jnp.where(blks[g] == 0, 1, 0)]
        for hp in range(N_PAIRS_A):
            sl = slice(hp * LANES, (hp + 1) * LANES)
            q2 = q_ref[:, sl] * scale
            kcat = jnp.concatenate([kring_refs[g][slots[g], :, sl], k_ref[:, sl]], axis=0)
            rhs = jnp.concatenate([kcat, bias_t], axis=1)
            for hh in range(2):
                qm = jnp.where(low_head == (hh == 0), q2, zero)
                s_ref[(g * N_PAIRS_A + hp) * 2 + hh] = _dot_nt(jnp.concatenate([qm, eye], axis=1), rhs)

    for pair in range(nb * N_PAIRS_A):
        ms, ls = [], []
        for u in (2 * pair, 2 * pair + 1):
            s = s_ref[u]
            m = jnp.max(s, axis=-1, keepdims=True)
            p = jnp.exp(s - m)
            p_ref[u] = p.astype(BF16)
            ms.append(m)
            ls.append(jnp.sum(p, axis=-1, keepdims=True))
        l = jnp.where(low_head, ls[0], ls[1])
        inv_ref[pair] = 1.0 / l
        stat_ref[pair] = jnp.where(low_head, ms[0], ms[1]) + jnp.log(l)

    for g, d in enumerate(DILATIONS):
        v_ref = tile(in_refs[3 * g + 2], d)
        if d == 1:
            rows = pl.ds(pl.multiple_of(r * ATT_BLK, ATT_BLK), ATT_BLK)
        else:
            rows = pl.ds((r // d) * (ATT_BLK * d) + r % d, ATT_BLK, stride=d)
        for hp in range(N_PAIRS_A):
            sl = slice(hp * LANES, (hp + 1) * LANES)
            vcat = jnp.concatenate([vring_refs[g][slots[g], :, sl], v_ref[:, sl]], axis=0)
            pair = g * N_PAIRS_A + hp
            pv = jnp.where(low_head, _dot(p_ref[2 * pair], vcat), _dot(p_ref[2 * pair + 1], vcat))
            acc_refs[g][hp, rows, :] = pv * inv_ref[pair]
            lse_refs[g][hp, rows, :] = stat_ref[pair]
        kring_refs[g][slots[g]] = tile(in_refs[3 * g + 1], d)[...]
        vring_refs[g][slots[g]] = v_ref[...]


def _attention(qkv, batch, seq):
    n_res = DILATIONS[-1]
    n_span = seq // ATT_SPAN
    nd = len(DILATIONS)
    operands, in_specs = [], []
    res = ATT_RES_PER_STEP
    for di, d in enumerate(DILATIONS):
        per = n_res // d
        assert d == 1 or d % res == 0
        if d == 1:
            shape = (1, 1, res * ATT_BLK, A_WIDTH)
            cur = lambda b, n, rs, per=per: (0, b, (n * per) // res + rs, 0)
        else:
            shape = (res, 1, ATT_BLK, A_WIDTH)
            cur = lambda b, n, rs, d=d, per=per: (rs % (d // res), b, n * per + (rs * res) // d, 0)
        operands += [qkv[which * nd + di].reshape(d, batch, seq // d, A_WIDTH) for which in range(3)]
        in_specs += [pl.BlockSpec(shape, cur)] * 3
    bias = jnp.asarray(_attn_bias(), BF16)
    n_units = len(DILATIONS) * N_HEADS_A
    rings = [pltpu.VMEM((d, ATT_BLK, A_WIDTH), BF16) for d in DILATIONS]
    out = pl.pallas_call(
        _attn_kernel,
        grid=(batch, n_span, n_res // res),
        in_specs=in_specs + [_const_spec(bias.shape)],
        out_specs=pl.BlockSpec((1, ATT_SPAN, A_WIDTH), lambda b, n, rs: (b, n, 0)),
        out_shape=jax.ShapeDtypeStruct((batch, seq, A_WIDTH), BF16),
        scratch_shapes=[pltpu.VMEM((N_PAIRS_A, ATT_SPAN, LANES), F32)] * (2 * len(DILATIONS))
        + [pltpu.VMEM((n_units, ATT_BLK, 2 * ATT_BLK), F32), pltpu.VMEM((n_units, ATT_BLK, 2 * ATT_BLK), BF16),
           pltpu.VMEM((n_units // 2, ATT_BLK, LANES), F32), pltpu.VMEM((n_units // 2, ATT_BLK, LANES), F32)]
        + rings + rings,
        compiler_params=_params(("arbitrary", "arbitrary", "arbitrary")),
        name="dilated_attention",
    )(*operands, bias)
    return out.reshape(batch * seq, A_WIDTH)


def _gla_scratch(tm):
    n_chunks = tm // GLA_CHUNK
    return ([pltpu.VMEM((tm, LANES), BF16)] * 4
            + [pltpu.VMEM((n_chunks, LANES), F32), pltpu.VMEM((n_chunks, 2 * DV_B, LANES), F32),
               pltpu.VMEM((n_chunks, 2 * DV_B, LANES), BF16), pltpu.VMEM((n_chunks, GLA_CHUNK, 2 * DV_B), F32),
               pltpu.VMEM((n_chunks, GLA_CHUNK, 2 * GLA_CHUNK), BF16)])


def _gla_phases(q_ref, k_ref, la_ref, v_ref, r_ref, g_ref, o_ref, state_ref,
                qi_ref, ki_ref, kd_ref, qg_ref, dec_ref, kv_ref, sb_ref, oi_ref, att_ref):
    n_chunks = q_ref.shape[0] // GLA_CHUNK
    c = GLA_CHUNK
    two = 2 * c

    def operands():
        row2 = lax.broadcasted_iota(jnp.int32, (two, two), 0)
        col2 = lax.broadcasted_iota(jnp.int32, (two, two), 1)
        tril2 = jnp.where((col2 <= row2) & (col2 // c == row2 // c), 1.0, 0.0).astype(BF16)
        for gi in range(n_chunks // 2):
            _gla_operands(gi, tril2, q_ref, k_ref, la_ref, qi_ref, ki_ref, kd_ref, qg_ref, dec_ref)

    lane_k = lax.broadcasted_iota(jnp.int32, (c, LANES), 1)
    lane_v = lax.broadcasted_iota(jnp.int32, (c, 2 * DV_B), 1)
    zero = jnp.zeros((), BF16)

    def scores():
        att_row = lax.broadcasted_iota(jnp.int32, (c, 2 * c), 0)
        att_col = lax.broadcasted_iota(jnp.int32, (c, 2 * c), 1)
        causal = (att_col % c) <= att_row
        inc_row = lax.broadcasted_iota(jnp.int32, (2 * DV_B, LANES), 0)
        inc_col = lax.broadcasted_iota(jnp.int32, (2 * DV_B, LANES), 1)
        own_block = (inc_row // DV_B) == (inc_col // DK_B)
        for ci in range(n_chunks):
            rows = slice(ci * c, (ci + 1) * c)
            ki = ki_ref[rows, :]
            k_stack = jnp.concatenate([jnp.where(lane_k < DK_B, ki, zero), jnp.where(lane_k >= DK_B, ki, zero)],
                                      axis=0)
            att = jnp.where(causal, _dot_nt(qi_ref[rows, :], k_stack), 0.0)
            att_ref[ci] = att.astype(BF16)
            v_t = v_ref[rows, :].astype(F32).T.astype(BF16)
            kv_ref[ci] = jnp.where(own_block, _dot(v_t, kd_ref[rows, :]), 0.0)

    def intra():
        for ci in range(n_chunks):
            v = v_ref[ci * c:(ci + 1) * c, :]
            v_blocks = jnp.concatenate([jnp.where(lane_v < DV_B, v, zero), jnp.where(lane_v >= DV_B, v, zero)],
                                       axis=0)
            oi_ref[ci] = _dot(att_ref[ci], v_blocks)

    def scan():
        state = state_ref[...]
        for ci in range(n_chunks):
            sb_ref[ci] = state.astype(BF16)
            state = state * dec_ref[ci:ci + 1, :] + kv_ref[ci]
        state_ref[...] = state

    def outputs():
        gain = g_ref[...]
        for ci in range(n_chunks):
            rows = slice(ci * c, (ci + 1) * c)
            o = oi_ref[ci] + _dot_nt(qg_ref[rows, :], sb_ref[ci])
            rg = r_ref[rows, :]
            gate = rg * jax.nn.sigmoid(rg)
            for h in range(2):
                cols = slice(h * DV_B, (h + 1) * DV_B)
                o_ref[rows, cols] = (_rms(o[:, cols]) * gain * gate[:, cols]).astype(BF16)

    return operands, scores, intra, scan, outputs


def _gla_operands(gi, tril2, q_ref, k_ref, la_ref, qi_ref, ki_ref, kd_ref, qg_ref, dec_ref):
    c = GLA_CHUNK
    two = 2 * c
    rows = slice(gi * two, (gi + 1) * two)
    la = la_ref[rows, :]
    hi = la.astype(BF16)
    rem = la - hi.astype(F32)
    mid = rem.astype(BF16)
    lo = (rem - mid.astype(F32)).astype(BF16)
    b3 = _dot(tril2, jnp.concatenate([hi, mid, lo], axis=1))
    b = b3[:, :LANES] + b3[:, LANES:2 * LANES] + b3[:, 2 * LANES:]
    b_last = jnp.concatenate([jnp.broadcast_to(b[c - 1:c], (c, LANES)),
                              jnp.broadcast_to(b[two - 1:two], (c, LANES))], axis=0)
    b_mid = jnp.concatenate([jnp.broadcast_to(b[c // 2 - 1:c // 2], (c, LANES)),
                             jnp.broadcast_to(b[c + c // 2 - 1:c + c // 2], (c, LANES))], axis=0)
    q = q_ref[rows, :] * (DK_B ** -0.5)
    k = k_ref[rows, :]
    qi_ref[rows, :] = (q * jnp.exp(b - b_mid)).astype(BF16)
    ki_ref[rows, :] = (k * jnp.exp(b_mid - b)).astype(BF16)
    kd_ref[rows, :] = (k * jnp.exp(b_last - b)).astype(BF16)
    qg_ref[rows, :] = (q * jnp.exp(b)).astype(BF16)
    dec_ref[2 * gi:2 * gi + 1, :] = jnp.exp(b[c - 1:c])
    dec_ref[2 * gi + 1:2 * gi + 2, :] = jnp.exp(b[two - 1:two])


def _sgu_kernel(h_ref, g_ref, w_in_ref, g_sgu_ref, ws_ref, bias_ref, o1_ref, o2_ref, hn_ref, v_ref, u_ref, *,
                n_chunks):
    d = D_MODEL
    hn_ref[...] = (_rms(h_ref[...]) * g_ref[...]).astype(BF16)
    v = jax.nn.gelu(_dot(hn_ref[...], w_in_ref[:, d:]))
    for cc in range(d // MXU_N):
        cols = slice(cc * MXU_N, (cc + 1) * MXU_N)
        u_ref[:, cols] = jax.nn.gelu(_dot(hn_ref[...], w_in_ref[:, cols]))
    v_ref[...] = (_rms(v) * g_sgu_ref[...]).astype(BF16)
    row = lax.broadcasted_iota(jnp.int32, (SGU_CHUNK, SGU_CHUNK), 0)
    col = lax.broadcasted_iota(jnp.int32, (SGU_CHUNK, SGU_CHUNK), 1)
    tril = col <= row
    half = d // 2
    per_dot = MXU_N // SGU_CHUNK
    for g in range(N_GROUPS_C):
        ws = jnp.where(tril, ws_ref[g], 0.0).astype(BF16)
        cols = slice(g * SGU_CHUNK, (g + 1) * SGU_CHUNK)
        for cc in range(n_chunks // per_dot):
            chunks = range(cc * per_dot, (cc + 1) * per_dot)
            v_cat = jnp.concatenate([v_ref[ci * SGU_CHUNK:(ci + 1) * SGU_CHUNK, cols] for ci in chunks], axis=1)
            sv_cat = _dot(ws, v_cat)
            for k, ci in enumerate(chunks):
                rows = slice(ci * SGU_CHUNK, (ci + 1) * SGU_CHUNK)
                sv = sv_cat[:, k * SGU_CHUNK:(k + 1) * SGU_CHUNK] + bias_ref[:, cols]
                y = (u_ref[rows, cols] * sv).astype(BF16)
                if g < N_GROUPS_C // 2:
                    o1_ref[rows, cols] = y
                else:
                    o2_ref[rows, g * SGU_CHUNK - half:(g + 1) * SGU_CHUNK - half] = y


def _sgu(h, g, w_in, g_sgu, ws, layer, bias, tm=512):
    t = h.shape[0]
    row = lambda n: pl.BlockSpec((tm, n), lambda i: (i, 0))
    half = D_MODEL // 2
    return pl.pallas_call(
        functools.partial(_sgu_kernel, n_chunks=tm // SGU_CHUNK),
        grid=(t // tm,),
        in_specs=[row(D_MODEL), _const_spec((1, D_MODEL)), _layer_spec(w_in, layer),
                  _const_spec((1, D_MODEL)), _layer_spec(ws, layer), _const_spec(bias.shape)],
        out_specs=[row(half), row(half)],
        out_shape=[jax.ShapeDtypeStruct((t, half), BF16)] * 2,
        scratch_shapes=[pltpu.VMEM((tm, D_MODEL), BF16), pltpu.VMEM((tm, D_MODEL), BF16),
                        pltpu.VMEM((tm, D_MODEL), F32)],
        compiler_params=_params(("arbitrary",)),
        name="sgu",
    )(h, g, w_in, g_sgu, ws, bias)


def _shift_rows(z, prev, k):
    rolled = pltpu.roll(z, k, axis=0)
    head = jnp.where(lax.broadcasted_iota(jnp.int32, prev.shape, 0) < k,
                     pltpu.roll(prev, k, axis=0), rolled[:8])
    return jnp.concatenate([head, rolled[8:]], axis=0)


def _ffn_kernel(h_ref, o1_ref, o2_ref, w_o_ref, g_ref, w_up_ref, cw_ref, cb_ref, w_down_ref, gf_ref,
                out_ref, hn_ref, act_ref, carry_ref, *, final_norm):
    @pl.when(pl.program_id(1) == 0)
    def _():
        carry_ref[...] = jnp.zeros_like(carry_ref)

    tm = h_ref.shape[0]
    o = jnp.concatenate([o1_ref[...], o2_ref[...]], axis=1)
    h1 = h_ref[...] + _dot(o, w_o_ref[...])
    out_ref[...] = h1
    hn_ref[...] = (_rms(h1) * g_ref[...]).astype(BF16)

    def conv(c, part):
        off = part * D_FF + c * FF_CHUNK
        idx = part * N_FF_CHUNKS + c
        z = _dot(hn_ref[...], w_up_ref[:, off:off + FF_CHUNK])
        prev = carry_ref[idx]
        carry_ref[idx] = z[tm - 8:]
        w = cw_ref[:, off:off + FF_CHUNK]
        acc = _shift_rows(z, prev, 2) * w[0:1] + _shift_rows(z, prev, 1) * w[1:2] + z * w[2:3]
        return acc + cb_ref[:, off:off + FF_CHUNK]

    for c in range(N_FF_CHUNKS):
        gate = conv(c, 0)
        up = conv(c, 1)
        act_ref[:, c * FF_CHUNK:(c + 1) * FF_CHUNK] = (gate * jax.nn.sigmoid(gate) * up).astype(BF16)
    h2 = out_ref[...] + _dot(act_ref[...], w_down_ref[...])
    if final_norm:
        h2 = _rms(h2) * gf_ref[...]
    out_ref[...] = h2


def _ffn(h, o1, o2, w_o, mixer_layer, g, w_up, conv_w, conv_b, w_down, layer, g_final, batch, seq, final_norm,
         tm=1024):
    t = batch * seq
    per = seq // tm
    row = lambda n: pl.BlockSpec((tm, n), lambda b, i: (b * per + i, 0))
    return pl.pallas_call(
        functools.partial(_ffn_kernel, final_norm=final_norm),
        grid=(batch, per),
        in_specs=[row(D_MODEL), row(D_MODEL // 2), row(D_MODEL // 2), _layer_spec(w_o, mixer_layer),
                  _const_spec((1, D_MODEL)), _layer_spec(w_up, layer), _layer_spec(conv_w, layer),
                  _const_spec(conv_b.shape), _layer_spec(w_down, layer), _const_spec((1, D_MODEL))],
        out_specs=row(D_MODEL),
        out_shape=jax.ShapeDtypeStruct((t, D_MODEL), F32),
        scratch_shapes=[pltpu.VMEM((tm, D_MODEL), BF16), pltpu.VMEM((tm, D_FF), BF16),
                        pltpu.VMEM((2 * N_FF_CHUNKS, 8, FF_CHUNK), F32)],
        compiler_params=_params(("arbitrary", "arbitrary")),
        name="ffn",
    )(h, o1, o2, w_o, g, w_up, conv_w, conv_b, w_down, g_final)


def kernel(x, positions, norm_mix, norm_ffn, w_in_ab, w_gate_up, b_gate_up, g_out_b, w_out_ab, w_in_c,
           g_sgu, w_spatial, b_spatial, w_out_c, w_up, conv_w, conv_b, w_down, norm_final):
    batch, seq, d = x.shape
    depth = norm_mix.shape[0]
    t = batch * seq
    h = x.reshape(t, d)
    cos, sin = _rope_tables(positions)
    row = lambda v: v.reshape(1, -1)
    b_end = 3 * A_WIDTH + 2 * B_QK + 2 * B_V
    w_in_ab_b = w_in_ab[:, :, :b_end].astype(BF16)
    w_gate_b = jnp.pad(w_in_ab[:, :, b_end:], ((0, 0), (0, 0), (0, LANES - GATE_RANK))).astype(BF16)
    w_gate_up_b = jnp.pad(w_gate_up, ((0, 0), (0, LANES - GATE_RANK), (0, 0))).astype(BF16)
    w_in_c_b = w_in_c.astype(BF16)
    w_out_b = (w_out_ab.astype(BF16), w_out_c.astype(BF16))
    w_up_b, w_down_b = w_up.astype(BF16), w_down.astype(BF16)
    for layer in range(depth):
        i = layer // 2
        if layer % 2 == 0:
            qkv, o2 = _proj_ab(h, row(norm_mix[layer]), w_in_ab_b, w_gate_b, i, w_gate_up_b, row(b_gate_up[i]), cos, sin,
                               row(g_out_b[i]), seq)
            o1 = _attention(qkv, batch, seq)
        else:
            bias = jnp.repeat(b_spatial[i].T, SGU_CHUNK, axis=1)
            o1, o2 = _sgu(h, row(norm_mix[layer]), w_in_c_b, row(g_sgu[i]), w_spatial, i, bias)
        h = _ffn(h, o1, o2, w_out_b[layer % 2], i, row(norm_ffn[layer]), w_up_b, conv_w, row(conv_b[layer]),
                 w_down_b, layer, row(norm_final), batch, seq, final_norm=(layer == depth - 1))
    return h.reshape(batch, seq, d)
```

```python
import functools

import numpy as np
import jax
import jax.numpy as jnp
from jax import lax
from jax.experimental import pallas as pl
from jax.experimental.pallas import tpu as pltpu

F32 = jnp.float32
BF16 = jnp.bfloat16

D_MODEL = 1024
N_HEADS_A = 8
HEAD_DIM_A = 64
ROT_DIM = HEAD_DIM_A // 4
ROPE_THETA = 500000.0
DILATIONS = (1, 4, 16)
ATT_BLK = 128
ATT_SPAN = ATT_BLK * DILATIONS[-1]
ATT_RES_PER_STEP = 2
N_HEADS_B = 4
DV_B = 128
DK_B = 64
GATE_RANK = 16
GATE_TAU = 16.0
GLA_CHUNK = 64
N_GROUPS_C = 8
SGU_CHUNK = 128
D_FF = 2816
CONV_W = 3
A_WIDTH = N_HEADS_A * HEAD_DIM_A
B_QK = N_HEADS_B * DK_B
B_V = N_HEADS_B * DV_B
NEG = -1e30
EPS = 1e-6
LANES = 128
N_PAIRS_A = A_WIDTH // LANES
MXU_N = 256
FF_CHUNK = MXU_N
N_FF_CHUNKS = D_FF // FF_CHUNK

VMEM_LIMIT = 56 * 1024 * 1024


def _dot(a, b):
    return jnp.dot(a, b, preferred_element_type=F32)


def _dot_nt(a, b):
    return lax.dot_general(a, b, (((1,), (1,)), ((), ())), preferred_element_type=F32)


def _rms(x):
    return x * lax.rsqrt(jnp.mean(x * x, axis=-1, keepdims=True) + EPS)


def _const_spec(shape):
    nd = len(shape)
    return pl.BlockSpec(shape, lambda *_: (0,) * nd, pipeline_mode=pl.Buffered(1))


def _layer_spec(stacked, layer):
    nd = stacked.ndim - 1
    return pl.BlockSpec((None,) + stacked.shape[1:], lambda *_: (layer,) + (0,) * nd, pipeline_mode=pl.Buffered(1))


def _params(sem, vmem=VMEM_LIMIT):
    return pltpu.CompilerParams(dimension_semantics=sem, vmem_limit_bytes=vmem)


def _rope_tab_kernel(pos_ref, inv_ref, sgn_ref, c_ref, s_ref):
    ang = pos_ref[...].astype(F32) * inv_ref[...]
    c_ref[...] = jnp.cos(ang)
    s_ref[...] = sgn_ref[...] * jnp.sin(ang)


def _rope_tables(positions):
    t = positions.size
    half = ROT_DIM // 2
    inv = np.float64(ROPE_THETA) ** (-np.arange(half, dtype=np.float64) * (2.0 / ROT_DIM))
    dim = np.arange(LANES) % HEAD_DIM_A
    inv_lane = np.where(dim < ROT_DIM, inv[dim % half], 0.0).astype(np.float32)[None, :]
    sgn_lane = np.where(dim < half, -1.0, np.where(dim < ROT_DIM, 1.0, 0.0)).astype(np.float32)[None, :]
    tm = 1024
    pos = positions.reshape(t, 1)
    return pl.pallas_call(
        _rope_tab_kernel,
        grid=(t // tm,),
        in_specs=[pl.BlockSpec((tm, 1), lambda i: (i, 0)),
                  _const_spec((1, LANES)), _const_spec((1, LANES))],
        out_specs=[pl.BlockSpec((tm, LANES), lambda i: (i, 0))] * 2,
        out_shape=[jax.ShapeDtypeStruct((t, LANES), F32)] * 2,
        compiler_params=_params(("arbitrary",)),
        name="rope_tables",
    )(pos, jnp.asarray(inv_lane), jnp.asarray(sgn_lane))


def _split_w_in_kernel(w_ref, main_ref, gate_ref):
    w = w_ref[0]
    n_main = main_ref.shape[-1]
    main_ref[0] = w[:, :n_main].astype(BF16)
    gate = w[:, n_main:]
    pad = jnp.zeros((gate.shape[0], LANES - gate.shape[1]), F32)
    gate_ref[0] = jnp.concatenate([gate, pad], axis=1).astype(BF16)


def _split_w_in(w_in_ab, tr=256):
    n_layers, d, width = w_in_ab.shape
    n_main = width - GATE_RANK
    return pl.pallas_call(
        _split_w_in_kernel,
        grid=(n_layers, d // tr),
        in_specs=[pl.BlockSpec((1, tr, width), lambda l, i: (l, i, 0))],
        out_specs=[pl.BlockSpec((1, tr, n_main), lambda l, i: (l, i, 0)),
                   pl.BlockSpec((1, tr, LANES), lambda l, i: (l, i, 0))],
        out_shape=[jax.ShapeDtypeStruct((n_layers, d, n_main), BF16),
                   jax.ShapeDtypeStruct((n_layers, d, LANES), BF16)],
        compiler_params=_params(("arbitrary", "arbitrary")),
        name="split_w_in",
    )(w_in_ab)


def _proj_ab_kernel(h_ref, g_ref, w_ref, wg_ref, wgu_ref, bgu_ref, c_ref, s_ref, gb_ref, *refs, tiles_per_seq):
    nd = len(DILATIONS)
    qkv_refs = refs[:3 * nd]
    ob_ref, hn_ref, stage_ref, stage2_ref, qb_ref, kb_ref, vb_ref, rb_ref, la_ref, state_ref = refs[3 * nd:3 * nd + 10]
    gla_scratch = refs[3 * nd + 10:]
    assert nd <= 3
    tm = h_ref.shape[0]
    hn_ref[...] = (_rms(h_ref[...]) * g_ref[...]).astype(BF16)
    cos = c_ref[...]
    sin = s_ref[...]
    lane = lax.broadcasted_iota(jnp.int32, cos.shape, 1)
    first_half = (lane % HEAD_DIM_A) < (ROT_DIM // 2)

    def rope(x):
        partner = jnp.where(first_half,
                            pltpu.roll(x, LANES - ROT_DIM // 2, axis=1),
                            pltpu.roll(x, ROT_DIM // 2, axis=1))
        return x * cos + partner * sin

    def project_a(which):
        for jj in range(A_WIDTH // MXU_N):
            col = which * A_WIDTH + jj * MXU_N
            x2 = _dot(hn_ref[...], w_ref[:, col:col + MXU_N])
            for half in range(MXU_N // LANES):
                j = jj * (MXU_N // LANES) + half
                x = x2[:, half * LANES:(half + 1) * LANES]
                if which < 2:
                    x = rope(x)
                qkv_refs[which * nd][:, j * LANES:(j + 1) * LANES] = x.astype(BF16)
                stage_ref[which * N_PAIRS_A + j] = x

    @pl.when(pl.program_id(0) % tiles_per_seq == 0)
    def _():
        state_ref[...] = jnp.zeros_like(state_ref)

    b0 = 3 * A_WIDTH
    qb_ref[...] = _dot(hn_ref[...], w_ref[:, b0:b0 + B_QK])
    kb_ref[...] = _dot(hn_ref[...], w_ref[:, b0 + B_QK:b0 + 2 * B_QK])
    vb_ref[...] = _dot(hn_ref[...], w_ref[:, b0 + 2 * B_QK:b0 + 2 * B_QK + B_V]).astype(BF16)
    rb_ref[...] = _dot(hn_ref[...], w_ref[:, b0 + 2 * B_QK + B_V:b0 + 2 * B_QK + 2 * B_V])
    gl = _dot(hn_ref[...], wg_ref[...])
    g = _dot(gl.astype(BF16), wgu_ref[...]) + bgu_ref[...]
    la_ref[...] = (jnp.minimum(g, 0.0) - jnp.log1p(jnp.exp(-jnp.abs(g)))) * (1.0 / GATE_TAU)
    n_scr = len(gla_scratch) // (N_HEADS_B // 2)
    pairs = []
    for p in range(N_HEADS_B // 2):
        qk = pl.ds(p * LANES, LANES)
        vr = pl.ds(p * 2 * DV_B, 2 * DV_B)
        pairs.append(_gla_phases(qb_ref.at[:, qk], kb_ref.at[:, qk], la_ref.at[:, qk], vb_ref.at[:, vr],
                                 rb_ref.at[:, vr], gb_ref, ob_ref.at[:, vr], state_ref.at[p],
                                 *gla_scratch[p * n_scr:(p + 1) * n_scr]))
    project_a(0)
    for operands, _, _, _, _ in pairs:
        operands()
    project_a(1)
    for _, scores, _, _, _ in pairs:
        scores()
    project_a(2)
    for _, _, intra, scan, _ in pairs:
        intra()
        scan()
    for _, _, _, _, outputs in pairs:
        outputs()
    for which in range(3):
        for j in range(N_PAIRS_A):
            u = which * N_PAIRS_A + j
            planes = [stage_ref.at[u]]
            for di in range(1, nd):
                ratio = DILATIONS[di] // DILATIONS[di - 1]
                n_rows = tm // DILATIONS[di]
                nxt = []
                for a in range(ratio):
                    for b, plane in enumerate(planes):
                        r = a * len(planes) + b
                        piece = plane[pl.ds(a, n_rows, stride=ratio), :]
                        qkv_refs[which * nd + di][r, :, j * LANES:(j + 1) * LANES] = piece.astype(BF16)
                        if di + 1 < nd:
                            plane_ref = stage2_ref.at[u, r]
                            plane_ref[...] = piece
                            nxt.append((r, plane_ref))
                planes = [p for _, p in sorted(nxt, key=lambda t: t[0])]


def _proj_ab(h, g, w_in, wg, layer, wgu, bgu, cos, sin, g_out, seq, tm=512):
    t = h.shape[0]
    row = lambda n: pl.BlockSpec((tm, n), lambda i: (i, 0))
    specs = [row(A_WIDTH)] + [pl.BlockSpec((d, tm // d, A_WIDTH), lambda i: (0, i, 0)) for d in DILATIONS[1:]]
    shapes = [jax.ShapeDtypeStruct((t, A_WIDTH), BF16)]
    shapes += [jax.ShapeDtypeStruct((d, t // d, A_WIDTH), BF16) for d in DILATIONS[1:]]
    specs, shapes = specs * 3, shapes * 3
    n_pairs = N_HEADS_B // 2
    outs = pl.pallas_call(
        functools.partial(_proj_ab_kernel, tiles_per_seq=seq // tm),
        grid=(t // tm,),
        in_specs=[row(D_MODEL), _const_spec((1, D_MODEL)), _layer_spec(w_in, layer), _layer_spec(wg, layer),
                  _layer_spec(wgu, layer), _const_spec(bgu.shape), row(LANES), row(LANES), _const_spec((1, DV_B))],
        out_specs=specs + [row(B_V)],
        out_shape=shapes + [jax.ShapeDtypeStruct((t, B_V), BF16)],
        scratch_shapes=[pltpu.VMEM((tm, D_MODEL), BF16), pltpu.VMEM((3 * N_PAIRS_A, tm, LANES), F32),
                        pltpu.VMEM((3 * N_PAIRS_A, DILATIONS[1], tm // DILATIONS[1], LANES), F32),
                        pltpu.VMEM((tm, B_QK), F32), pltpu.VMEM((tm, B_QK), F32), pltpu.VMEM((tm, B_V), BF16),
                        pltpu.VMEM((tm, B_V), F32), pltpu.VMEM((tm, B_QK), F32),
                        pltpu.VMEM((n_pairs, 2 * DV_B, LANES), F32)] + _gla_scratch(tm) * n_pairs,
        compiler_params=_params(("arbitrary",)),
        name="proj_ab",
    )(h, g, w_in, wg, wgu, bgu, cos, sin, g_out)
    return outs[:3 * len(DILATIONS)], outs[3 * len(DILATIONS)]


def _attn_bias():
    a = np.arange(ATT_BLK)[None, :]
    c = np.arange(2 * ATT_BLK)[:, None]
    band = (c >= a) & (c <= a + ATT_BLK)
    first = band & (c >= ATT_BLK)
    return np.where(np.stack([band, first]), 0.0, NEG).astype(np.float32)


def _attn_kernel(*refs):
    nb = len(DILATIONS)
    in_refs = refs[:3 * nb]
    bias_ref = refs[3 * nb]
    out_ref = refs[3 * nb + 1]
    scratch = refs[3 * nb + 2:]
    acc_refs, lse_refs = scratch[:nb], scratch[nb:2 * nb]
    s_ref, p_ref, inv_ref, stat_ref = scratch[2 * nb:2 * nb + 4]
    kring_refs, vring_refs = scratch[2 * nb + 4:3 * nb + 4], scratch[3 * nb + 4:]
    n = pl.program_id(1)
    n_res = DILATIONS[-1]
    for sub in range(ATT_RES_PER_STEP):
        _attn_substep(n, pl.program_id(2) * ATT_RES_PER_STEP + sub, sub, in_refs, bias_ref, acc_refs, lse_refs,
                      s_ref, p_ref, inv_ref, stat_ref, kring_refs, vring_refs)

    @pl.when(pl.program_id(2) == n_res // ATT_RES_PER_STEP - 1)
    def _():
        chunk = 256
        for hp in range(A_WIDTH // LANES):
            def body(i, carry):
                rows = pl.ds(pl.multiple_of(i * chunk, chunk), chunk)
                lse = [ref[hp, rows, :] for ref in lse_refs]
                top = functools.reduce(jnp.maximum, lse)
                w = [jnp.exp(x - top) for x in lse]
                num = functools.reduce(jnp.add, [wi * ref[hp, rows, :] for wi, ref in zip(w, acc_refs)])
                out_ref[0, rows, hp * LANES:(hp + 1) * LANES] = (num / functools.reduce(jnp.add, w)).astype(BF16)
                return carry
            lax.fori_loop(0, ATT_SPAN // chunk, body, 0)


def _attn_substep(n, r, sub, in_refs, bias_ref, acc_refs, lse_refs, s_ref, p_ref, inv_ref, stat_ref,
                  kring_refs, vring_refs):
    nb = len(DILATIONS)
    n_res = DILATIONS[-1]

    def tile(ref, d):
        return ref.at[0, 0, pl.ds(sub * ATT_BLK, ATT_BLK)] if d == 1 else ref.at[sub, 0]

    lane = lax.broadcasted_iota(jnp.int32, (ATT_BLK, LANES), 1)
    low_head = lane < HEAD_DIM_A
    zero = jnp.zeros((), BF16)
    scale = jnp.asarray(HEAD_DIM_A ** -0.5, BF16)
    eye = jnp.where(lax.broadcasted_iota(jnp.int32, (ATT_BLK, ATT_BLK), 0)
                    == lax.broadcasted_iota(jnp.int32, (ATT_BLK, ATT_BLK), 1), 1.0, 0.0).astype(BF16)
    slots = [r % d for d in DILATIONS]
    blks = [n * (n_res // d) + r // d for d in DILATIONS]

    @pl.when(n == 0)
    def _():
        for g in range(nb):
            @pl.when(blks[g] == 0)
            def _():
                kring_refs[g][slots[g]] = jnp.zeros((ATT_BLK, A_WIDTH), BF16)
                vring_refs[g][slots[g]] = jnp.zeros((ATT_BLK, A_WIDTH), BF16)

    for g, d in enumerate(DILATIONS):
        q_ref, k_ref = tile(in_refs[3 * g], d), tile(in_refs[3 * g + 1], d)
        bias_t = bias_ref[jnp.where(blks[g] == 0, 1, 0)]
        for hp in range(N_PAIRS_A):
            sl = slice(hp * LANES, (hp + 1) * LANES)
            q2 = q_ref[:, sl] * scale
            kcat = jnp.concatenate([kring_refs[g][slots[g], :, sl], k_ref[:, sl]], axis=0)
            rhs = jnp.concatenate([kcat, bias_t], axis=1)
            for hh in range(2):
                qm = jnp.where(low_head == (hh == 0), q2, zero)
                s_ref[(g * N_PAIRS_A + hp) * 2 + hh] = _dot_nt(jnp.concatenate([qm, eye], axis=1), rhs)

    for pair in range(nb * N_PAIRS_A):
        ms, ls = [], []
        for u in (2 * pair, 2 * pair + 1):
            s = s_ref[u]
            m = jnp.max(s, axis=-1, keepdims=True)
            p = jnp.exp(s - m)
            p_ref[u] = p.astype(BF16)
            ms.append(m)
            ls.append(jnp.sum(p, axis=-1, keepdims=True))
        l = jnp.where(low_head, ls[0], ls[1])
        inv_ref[pair] = 1.0 / l
        stat_ref[pair] = jnp.where(low_head, ms[0], ms[1]) + jnp.log(l)

    for g, d in enumerate(DILATIONS):
        v_ref = tile(in_refs[3 * g + 2], d)
        if d == 1:
            rows = pl.ds(pl.multiple_of(r * ATT_BLK, ATT_BLK), ATT_BLK)
        else:
            rows = pl.ds((r // d) * (ATT_BLK * d) + r % d, ATT_BLK, stride=d)
        for hp in range(N_PAIRS_A):
            sl = slice(hp * LANES, (hp + 1) * LANES)
            vcat = jnp.concatenate([vring_refs[g][slots[g], :, sl], v_ref[:, sl]], axis=0)
            pair = g * N_PAIRS_A + hp
            pv = jnp.where(low_head, _dot(p_ref[2 * pair], vcat), _dot(p_ref[2 * pair + 1], vcat))
            acc_refs[g][hp, rows, :] = pv * inv_ref[pair]
            lse_refs[g][hp, rows, :] = stat_ref[pair]
        kring_refs[g][slots[g]] = tile(in_refs[3 * g + 1], d)[...]
        vring_refs[g][slots[g]] = v_ref[...]


def _attention(qkv, batch, seq):
    n_res = DILATIONS[-1]
    n_span = seq // ATT_SPAN
    nd = len(DILATIONS)
    operands, in_specs = [], []
    res = ATT_RES_PER_STEP
    for di, d in enumerate(DILATIONS):
        per = n_res // d
        assert d == 1 or d % res == 0
        if d == 1:
            shape = (1, 1, res * ATT_BLK, A_WIDTH)
            cur = lambda b, n, rs, per=per: (0, b, (n * per) // res + rs, 0)
        else:
            shape = (res, 1, ATT_BLK, A_WIDTH)
            cur = lambda b, n, rs, d=d, per=per: (rs % (d // res), b, n * per + (rs * res) // d, 0)
        operands += [qkv[which * nd + di].reshape(d, batch, seq // d, A_WIDTH) for which in range(3)]
        in_specs += [pl.BlockSpec(shape, cur)] * 3
    bias = jnp.asarray(_attn_bias(), BF16)
    n_units = len(DILATIONS) * N_HEADS_A
    rings = [pltpu.VMEM((d, ATT_BLK, A_WIDTH), BF16) for d in DILATIONS]
    out = pl.pallas_call(
        _attn_kernel,
        grid=(batch, n_span, n_res // res),
        in_specs=in_specs + [_const_spec(bias.shape)],
        out_specs=pl.BlockSpec((1, ATT_SPAN, A_WIDTH), lambda b, n, rs: (b, n, 0)),
        out_shape=jax.ShapeDtypeStruct((batch, seq, A_WIDTH), BF16),
        scratch_shapes=[pltpu.VMEM((N_PAIRS_A, ATT_SPAN, LANES), F32)] * (2 * len(DILATIONS))
        + [pltpu.VMEM((n_units, ATT_BLK, 2 * ATT_BLK), F32), pltpu.VMEM((n_units, ATT_BLK, 2 * ATT_BLK), BF16),
           pltpu.VMEM((n_units // 2, ATT_BLK, LANES), F32), pltpu.VMEM((n_units // 2, ATT_BLK, LANES), F32)]
        + rings + rings,
        compiler_params=_params(("arbitrary", "arbitrary", "arbitrary")),
        name="dilated_attention",
    )(*operands, bias)
    return out.reshape(batch * seq, A_WIDTH)


def _gla_scratch(tm):
    n_chunks = tm // GLA_CHUNK
    return ([pltpu.VMEM((tm, LANES), BF16)] * 4
            + [pltpu.VMEM((n_chunks, LANES), F32), pltpu.VMEM((n_chunks, 2 * DV_B, LANES), F32),
               pltpu.VMEM((n_chunks, 2 * DV_B, LANES), BF16), pltpu.VMEM((n_chunks, GLA_CHUNK, 2 * DV_B), F32),
               pltpu.VMEM((n_chunks, GLA_CHUNK, 2 * GLA_CHUNK), BF16)])


def _gla_phases(q_ref, k_ref, la_ref, v_ref, r_ref, g_ref, o_ref, state_ref,
                qi_ref, ki_ref, kd_ref, qg_ref, dec_ref, kv_ref, sb_ref, oi_ref, att_ref):
    n_chunks = q_ref.shape[0] // GLA_CHUNK
    c = GLA_CHUNK
    two = 2 * c

    def operands():
        row2 = lax.broadcasted_iota(jnp.int32, (two, two), 0)
        col2 = lax.broadcasted_iota(jnp.int32, (two, two), 1)
        tril2 = jnp.where((col2 <= row2) & (col2 // c == row2 // c), 1.0, 0.0).astype(BF16)
        for gi in range(n_chunks // 2):
            _gla_operands(gi, tril2, q_ref, k_ref, la_ref, qi_ref, ki_ref, kd_ref, qg_ref, dec_ref)

    lane_k = lax.broadcasted_iota(jnp.int32, (c, LANES), 1)
    lane_v = lax.broadcasted_iota(jnp.int32, (c, 2 * DV_B), 1)
    zero = jnp.zeros((), BF16)

    def scores():
        att_row = lax.broadcasted_iota(jnp.int32, (c, 2 * c), 0)
        att_col = lax.broadcasted_iota(jnp.int32, (c, 2 * c), 1)
        causal = (att_col % c) <= att_row
        inc_row = lax.broadcasted_iota(jnp.int32, (2 * DV_B, LANES), 0)
        inc_col = lax.broadcasted_iota(jnp.int32, (2 * DV_B, LANES), 1)
        own_block = (inc_row // DV_B) == (inc_col // DK_B)
        for ci in range(n_chunks):
            rows = slice(ci * c, (ci + 1) * c)
            ki = ki_ref[rows, :]
            k_stack = jnp.concatenate([jnp.where(lane_k < DK_B, ki, zero), jnp.where(lane_k >= DK_B, ki, zero)],
                                      axis=0)
            att = jnp.where(causal, _dot_nt(qi_ref[rows, :], k_stack), 0.0)
            att_ref[ci] = att.astype(BF16)
            v_t = v_ref[rows, :].astype(F32).T.astype(BF16)
            kv_ref[ci] = jnp.where(own_block, _dot(v_t, kd_ref[rows, :]), 0.0)

    def intra():
        for ci in range(n_chunks):
            v = v_ref[ci * c:(ci + 1) * c, :]
            v_blocks = jnp.concatenate([jnp.where(lane_v < DV_B, v, zero), jnp.where(lane_v >= DV_B, v, zero)],
                                       axis=0)
            oi_ref[ci] = _dot(att_ref[ci], v_blocks)

    def scan():
        state = state_ref[...]
        for ci in range(n_chunks):
            sb_ref[ci] = state.astype(BF16)
            state = state * dec_ref[ci:ci + 1, :] + kv_ref[ci]
        state_ref[...] = state

    def outputs():
        gain = g_ref[...]
        for ci in range(n_chunks):
            rows = slice(ci * c, (ci + 1) * c)
            o = oi_ref[ci] + _dot_nt(qg_ref[rows, :], sb_ref[ci])
            rg = r_ref[rows, :]
            gate = rg * jax.nn.sigmoid(rg)
            for h in range(2):
                cols = slice(h * DV_B, (h + 1) * DV_B)
                o_ref[rows, cols] = (_rms(o[:, cols]) * gain * gate[:, cols]).astype(BF16)

    return operands, scores, intra, scan, outputs


def _gla_operands(gi, tril2, q_ref, k_ref, la_ref, qi_ref, ki_ref, kd_ref, qg_ref, dec_ref):
    c = GLA_CHUNK
    two = 2 * c
    rows = slice(gi * two, (gi + 1) * two)
    la = la_ref[rows, :]
    hi = la.astype(BF16)
    rem = la - hi.astype(F32)
    mid = rem.astype(BF16)
    lo = (rem - mid.astype(F32)).astype(BF16)
    b3 = _dot(tril2, jnp.concatenate([hi, mid, lo], axis=1))
    b = b3[:, :LANES] + b3[:, LANES:2 * LANES] + b3[:, 2 * LANES:]
    b_last = jnp.concatenate([jnp.broadcast_to(b[c - 1:c], (c, LANES)),
                              jnp.broadcast_to(b[two - 1:two], (c, LANES))], axis=0)
    b_mid = jnp.concatenate([jnp.broadcast_to(b[c // 2 - 1:c // 2], (c, LANES)),
                             jnp.broadcast_to(b[c + c // 2 - 1:c + c // 2], (c, LANES))], axis=0)
    q = q_ref[rows, :] * (DK_B ** -0.5)
    k = k_ref[rows, :]
    qi_ref[rows, :] = (q * jnp.exp(b - b_mid)).astype(BF16)
    ki_ref[rows, :] = (k * jnp.exp(b_mid - b)).astype(BF16)
    kd_ref[rows, :] = (k * jnp.exp(b_last - b)).astype(BF16)
    qg_ref[rows, :] = (q * jnp.exp(b)).astype(BF16)
    dec_ref[2 * gi:2 * gi + 1, :] = jnp.exp(b[c - 1:c])
    dec_ref[2 * gi + 1:2 * gi + 2, :] = jnp.exp(b[two - 1:two])


def _sgu_kernel(h_ref, g_ref, w_in_ref, g_sgu_ref, ws_ref, bias_ref, o1_ref, o2_ref, hn_ref, v_ref, u_ref, *,
                n_chunks):
    d = D_MODEL
    hn_ref[...] = (_rms(h_ref[...]) * g_ref[...]).astype(BF16)
    v = jax.nn.gelu(_dot(hn_ref[...], w_in_ref[:, d:]))
    for cc in range(d // MXU_N):
        cols = slice(cc * MXU_N, (cc + 1) * MXU_N)
        u_ref[:, cols] = jax.nn.gelu(_dot(hn_ref[...], w_in_ref[:, cols]))
    v_ref[...] = (_rms(v) * g_sgu_ref[...]).astype(BF16)
    row = lax.broadcasted_iota(jnp.int32, (SGU_CHUNK, SGU_CHUNK), 0)
    col = lax.broadcasted_iota(jnp.int32, (SGU_CHUNK, SGU_CHUNK), 1)
    tril = col <= row
    half = d // 2
    per_dot = MXU_N // SGU_CHUNK
    for g in range(N_GROUPS_C):
        ws = jnp.where(tril, ws_ref[g], 0.0).astype(BF16)
        cols = slice(g * SGU_CHUNK, (g + 1) * SGU_CHUNK)
        for cc in range(n_chunks // per_dot):
            chunks = range(cc * per_dot, (cc + 1) * per_dot)
            v_cat = jnp.concatenate([v_ref[ci * SGU_CHUNK:(ci + 1) * SGU_CHUNK, cols] for ci in chunks], axis=1)
            sv_cat = _dot(ws, v_cat)
            for k, ci in enumerate(chunks):
                rows = slice(ci * SGU_CHUNK, (ci + 1) * SGU_CHUNK)
                sv = sv_cat[:, k * SGU_CHUNK:(k + 1) * SGU_CHUNK] + bias_ref[:, cols]
                y = (u_ref[rows, cols] * sv).astype(BF16)
                if g < N_GROUPS_C // 2:
                    o1_ref[rows, cols] = y
                else:
                    o2_ref[rows, g * SGU_CHUNK - half:(g + 1) * SGU_CHUNK - half] = y


def _sgu(h, g, w_in, g_sgu, ws, layer, bias, tm=512):
    t = h.shape[0]
    row = lambda n: pl.BlockSpec((tm, n), lambda i: (i, 0))
    half = D_MODEL // 2
    return pl.pallas_call(
        functools.partial(_sgu_kernel, n_chunks=tm // SGU_CHUNK),
        grid=(t // tm,),
        in_specs=[row(D_MODEL), _const_spec((1, D_MODEL)), _layer_spec(w_in, layer),
                  _const_spec((1, D_MODEL)), _layer_spec(ws, layer), _const_spec(bias.shape)],
        out_specs=[row(half), row(half)],
        out_shape=[jax.ShapeDtypeStruct((t, half), BF16)] * 2,
        scratch_shapes=[pltpu.VMEM((tm, D_MODEL), BF16), pltpu.VMEM((tm, D_MODEL), BF16),
                        pltpu.VMEM((tm, D_MODEL), F32)],
        compiler_params=_params(("arbitrary",)),
        name="sgu",
    )(h, g, w_in, g_sgu, ws, bias)


def _shift_rows(z, prev, k):
    rolled = pltpu.roll(z, k, axis=0)
    head = jnp.where(lax.broadcasted_iota(jnp.int32, prev.shape, 0) < k,
                     pltpu.roll(prev, k, axis=0), rolled[:8])
    return jnp.concatenate([head, rolled[8:]], axis=0)


def _ffn_kernel(h_ref, o1_ref, o2_ref, w_o_ref, g_ref, w_up_ref, cw_ref, cb_ref, w_down_ref, gf_ref,
                out_ref, hn_ref, act_ref, carry_ref, *, final_norm):
    @pl.when(pl.program_id(1) == 0)
    def _():
        carry_ref[...] = jnp.zeros_like(carry_ref)

    tm = h_ref.shape[0]
    o = jnp.concatenate([o1_ref[...], o2_ref[...]], axis=1)
    h1 = h_ref[...] + _dot(o, w_o_ref[...])
    out_ref[...] = h1
    hn_ref[...] = (_rms(h1) * g_ref[...]).astype(BF16)

    def conv(c, part):
        off = part * D_FF + c * FF_CHUNK
        idx = part * N_FF_CHUNKS + c
        z = _dot(hn_ref[...], w_up_ref[:, off:off + FF_CHUNK])
        prev = carry_ref[idx]
        carry_ref[idx] = z[tm - 8:]
        w = cw_ref[:, off:off + FF_CHUNK]
        acc = _shift_rows(z, prev, 2) * w[0:1] + _shift_rows(z, prev, 1) * w[1:2] + z * w[2:3]
        return acc + cb_ref[:, off:off + FF_CHUNK]

    for c in range(N_FF_CHUNKS):
        gate = conv(c, 0)
        up = conv(c, 1)
        act_ref[:, c * FF_CHUNK:(c + 1) * FF_CHUNK] = (gate * jax.nn.sigmoid(gate) * up).astype(BF16)
    h2 = out_ref[...] + _dot(act_ref[...], w_down_ref[...])
    if final_norm:
        h2 = _rms(h2) * gf_ref[...]
    out_ref[...] = h2


def _ffn(h, o1, o2, w_o, mixer_layer, g, w_up, conv_w, conv_b, w_down, layer, g_final, batch, seq, final_norm,
         tm=1024):
    t = batch * seq
    per = seq // tm
    row = lambda n: pl.BlockSpec((tm, n), lambda b, i: (b * per + i, 0))
    return pl.pallas_call(
        functools.partial(_ffn_kernel, final_norm=final_norm),
        grid=(batch, per),
        in_specs=[row(D_MODEL), row(D_MODEL // 2), row(D_MODEL // 2), _layer_spec(w_o, mixer_layer),
                  _const_spec((1, D_MODEL)), _layer_spec(w_up, layer), _layer_spec(conv_w, layer),
                  _const_spec(conv_b.shape), _layer_spec(w_down, layer), _const_spec((1, D_MODEL))],
        out_specs=row(D_MODEL),
        out_shape=jax.ShapeDtypeStruct((t, D_MODEL), F32),
        scratch_shapes=[pltpu.VMEM((tm, D_MODEL), BF16), pltpu.VMEM((tm, D_FF), BF16),
                        pltpu.VMEM((2 * N_FF_CHUNKS, 8, FF_CHUNK), F32)],
        compiler_params=_params(("arbitrary", "arbitrary")),
        name="ffn",
    )(h, o1, o2, w_o, g, w_up, conv_w, conv_b, w_down, g_final)


def kernel(x, positions, norm_mix, norm_ffn, w_in_ab, w_gate_up, b_gate_up, g_out_b, w_out_ab, w_in_c,
           g_sgu, w_spatial, b_spatial, w_out_c, w_up, conv_w, conv_b, w_down, norm_final):
    batch, seq, d = x.shape
    depth = norm_mix.shape[0]
    t = batch * seq
    h = x.reshape(t, d)
    cos, sin = _rope_tables(positions)
    row = lambda v: v.reshape(1, -1)
    w_in_ab_b, w_gate_b = _split_w_in(w_in_ab)
    w_gate_up_b = jnp.pad(w_gate_up, ((0, 0), (0, LANES - GATE_RANK), (0, 0))).astype(BF16)
    w_in_c_b = w_in_c.astype(BF16)
    w_out_b = (w_out_ab.astype(BF16), w_out_c.astype(BF16))
    w_up_b, w_down_b = w_up.astype(BF16), w_down.astype(BF16)
    for layer in range(depth):
        i = layer // 2
        if layer % 2 == 0:
            qkv, o2 = _proj_ab(h, row(norm_mix[layer]), w_in_ab_b, w_gate_b, i, w_gate_up_b, row(b_gate_up[i]), cos, sin,
                               row(g_out_b[i]), seq)
            o1 = _attention(qkv, batch, seq)
        else:
            bias = jnp.repeat(b_spatial[i].T, SGU_CHUNK, axis=1)
            o1, o2 = _sgu(h, row(norm_mix[layer]), w_in_c_b, row(g_sgu[i]), w_spatial, i, bias)
        h = _ffn(h, o1, o2, w_out_b[layer % 2], i, row(norm_ffn[layer]), w_up_b, conv_w, row(conv_b[layer]),
                 w_down_b, layer, row(norm_final), batch, seq, final_norm=(layer == depth - 1))
    return h.reshape(batch, seq, d)
```

```python
import functools

import numpy as np
import jax
import jax.numpy as jnp
from jax import lax
from jax.experimental import pallas as pl
from jax.experimental.pallas import tpu as pltpu

F32 = jnp.float32
BF16 = jnp.bfloat16

D_MODEL = 1024
N_HEADS_A = 8
HEAD_DIM_A = 64
ROT_DIM = HEAD_DIM_A // 4
ROPE_THETA = 500000.0
DILATIONS = (1, 4, 16)
ATT_BLK = 128
ATT_SPAN = ATT_BLK * DILATIONS[-1]
ATT_RES_PER_STEP = 4
N_HEADS_B = 4
DV_B = 128
DK_B = 64
GATE_RANK = 16
GATE_TAU = 16.0
GLA_CHUNK = 64
N_GROUPS_C = 8
SGU_CHUNK = 128
D_FF = 2816
CONV_W = 3
A_WIDTH = N_HEADS_A * HEAD_DIM_A
B_QK = N_HEADS_B * DK_B
B_V = N_HEADS_B * DV_B
NEG = -1e30
EPS = 1e-6
LANES = 128
SUBLANES = 8
N_PAIRS_A = A_WIDTH // LANES
MXU_N = 256
FF_CHUNK = MXU_N
N_FF_CHUNKS = D_FF // FF_CHUNK

VMEM_LIMIT = 56 * 1024 * 1024


def _dot(a, b):
    return jnp.dot(a, b, preferred_element_type=F32)


def _dot_nt(a, b):
    return lax.dot_general(a, b, (((1,), (1,)), ((), ())), preferred_element_type=F32)


def _rms(x):
    return x * lax.rsqrt(jnp.mean(x * x, axis=-1, keepdims=True) + EPS)


def _const_spec(shape):
    nd = len(shape)
    return pl.BlockSpec(shape, lambda *_: (0,) * nd, pipeline_mode=pl.Buffered(1))


def _layer_spec(stacked, layer):
    nd = stacked.ndim - 1
    return pl.BlockSpec((None,) + stacked.shape[1:], lambda *_: (layer,) + (0,) * nd, pipeline_mode=pl.Buffered(1))


def _params(sem, vmem=VMEM_LIMIT):
    return pltpu.CompilerParams(dimension_semantics=sem, vmem_limit_bytes=vmem)


def _rope_tab_kernel(pos_ref, inv_ref, sgn_ref, c_ref, s_ref):
    ang = pos_ref[...].astype(F32) * inv_ref[...]
    c_ref[...] = jnp.cos(ang)
    s_ref[...] = sgn_ref[...] * jnp.sin(ang)


def _rope_tables(positions):
    t = positions.size
    half = ROT_DIM // 2
    inv = np.float64(ROPE_THETA) ** (-np.arange(half, dtype=np.float64) * (2.0 / ROT_DIM))
    dim = np.arange(LANES) % HEAD_DIM_A
    inv_lane = np.where(dim < ROT_DIM, inv[dim % half], 0.0).astype(np.float32)[None, :]
    sgn_lane = np.where(dim < half, -1.0, np.where(dim < ROT_DIM, 1.0, 0.0)).astype(np.float32)[None, :]
    tm = 1024
    pos = positions.reshape(t, 1)
    return pl.pallas_call(
        _rope_tab_kernel,
        grid=(t // tm,),
        in_specs=[pl.BlockSpec((tm, 1), lambda i: (i, 0)),
                  _const_spec((1, LANES)), _const_spec((1, LANES))],
        out_specs=[pl.BlockSpec((tm, LANES), lambda i: (i, 0))] * 2,
        out_shape=[jax.ShapeDtypeStruct((t, LANES), F32)] * 2,
        compiler_params=_params(("arbitrary",)),
        name="rope_tables",
    )(pos, jnp.asarray(inv_lane), jnp.asarray(sgn_lane))


def _split_w_in_kernel(w_ref, main_ref, gate_ref):
    w = w_ref[0]
    n_main = main_ref.shape[-1]
    main_ref[0] = w[:, :n_main].astype(BF16)
    gate = w[:, n_main:]
    pad = jnp.zeros((gate.shape[0], LANES - gate.shape[1]), F32)
    gate_ref[0] = jnp.concatenate([gate, pad], axis=1).astype(BF16)


def _split_w_in(w_in_ab, tr=256):
    n_layers, d, width = w_in_ab.shape
    n_main = width - GATE_RANK
    return pl.pallas_call(
        _split_w_in_kernel,
        grid=(n_layers, d // tr),
        in_specs=[pl.BlockSpec((1, tr, width), lambda l, i: (l, i, 0))],
        out_specs=[pl.BlockSpec((1, tr, n_main), lambda l, i: (l, i, 0)),
                   pl.BlockSpec((1, tr, LANES), lambda l, i: (l, i, 0))],
        out_shape=[jax.ShapeDtypeStruct((n_layers, d, n_main), BF16),
                   jax.ShapeDtypeStruct((n_layers, d, LANES), BF16)],
        compiler_params=_params(("arbitrary", "arbitrary")),
        name="split_w_in",
    )(w_in_ab)


def _proj_ab_kernel(h_ref, g_ref, w_ref, wg_ref, wgu_ref, bgu_ref, c_ref, s_ref, gb_ref, *refs, tiles_per_seq):
    nd = len(DILATIONS)
    qkv_refs = refs[:3 * nd]
    ob_ref, hn_ref, stage_ref, stage2_ref, qb_ref, kb_ref, vb_ref, rb_ref, la_ref, state_ref = refs[3 * nd:3 * nd + 10]
    gla_scratch = refs[3 * nd + 10:]
    assert nd <= 3
    tm = h_ref.shape[0]
    hn_ref[...] = (_rms(h_ref[...]) * g_ref[...]).astype(BF16)
    cos = c_ref[...]
    sin = s_ref[...]
    lane = lax.broadcasted_iota(jnp.int32, cos.shape, 1)
    first_half = (lane % HEAD_DIM_A) < (ROT_DIM // 2)

    def rope(x):
        partner = jnp.where(first_half,
                            pltpu.roll(x, LANES - ROT_DIM // 2, axis=1),
                            pltpu.roll(x, ROT_DIM // 2, axis=1))
        return x * cos + partner * sin

    def project_a(which):
        for jj in range(A_WIDTH // MXU_N):
            col = which * A_WIDTH + jj * MXU_N
            x2 = _dot(hn_ref[...], w_ref[:, col:col + MXU_N])
            for half in range(MXU_N // LANES):
                j = jj * (MXU_N // LANES) + half
                x = x2[:, half * LANES:(half + 1) * LANES]
                if which < 2:
                    x = rope(x)
                qkv_refs[which * nd][:, j * LANES:(j + 1) * LANES] = x.astype(BF16)
                stage_ref[which * N_PAIRS_A + j] = x

    @pl.when(pl.program_id(0) % tiles_per_seq == 0)
    def _():
        state_ref[...] = jnp.zeros_like(state_ref)

    b0 = 3 * A_WIDTH
    qb_ref[...] = _dot(hn_ref[...], w_ref[:, b0:b0 + B_QK])
    kb_ref[...] = _dot(hn_ref[...], w_ref[:, b0 + B_QK:b0 + 2 * B_QK])
    vb_ref[...] = _dot(hn_ref[...], w_ref[:, b0 + 2 * B_QK:b0 + 2 * B_QK + B_V]).astype(BF16)
    rb_ref[...] = _dot(hn_ref[...], w_ref[:, b0 + 2 * B_QK + B_V:b0 + 2 * B_QK + 2 * B_V])
    gl = _dot(hn_ref[...], wg_ref[...])
    g = _dot(gl.astype(BF16), wgu_ref[...]) + bgu_ref[...]
    la_ref[...] = (jnp.minimum(g, 0.0) - jnp.log1p(jnp.exp(-jnp.abs(g)))) * (1.0 / GATE_TAU)
    n_scr = len(gla_scratch) // (N_HEADS_B // 2)
    pairs = []
    for p in range(N_HEADS_B // 2):
        qk = pl.ds(p * LANES, LANES)
        vr = pl.ds(p * 2 * DV_B, 2 * DV_B)
        pairs.append(_gla_phases(qb_ref.at[:, qk], kb_ref.at[:, qk], la_ref.at[:, qk], vb_ref.at[:, vr],
                                 rb_ref.at[:, vr], gb_ref, ob_ref.at[:, vr], state_ref.at[p],
                                 *gla_scratch[p * n_scr:(p + 1) * n_scr]))
    project_a(0)
    for operands, _, _, _, _ in pairs:
        operands()
    project_a(1)
    for _, scores, _, _, _ in pairs:
        scores()
    project_a(2)
    for _, _, intra, scan, _ in pairs:
        intra()
        scan()
    for _, _, _, _, outputs in pairs:
        outputs()
    for which in range(3):
        for j in range(N_PAIRS_A):
            u = which * N_PAIRS_A + j
            planes = [stage_ref.at[u]]
            for di in range(1, nd):
                ratio = DILATIONS[di] // DILATIONS[di - 1]
                n_rows = tm // DILATIONS[di]
                nxt = []
                for a in range(ratio):
                    for b, plane in enumerate(planes):
                        r = a * len(planes) + b
                        piece = plane[pl.ds(a, n_rows, stride=ratio), :]
                        qkv_refs[which * nd + di][r, :, j * LANES:(j + 1) * LANES] = piece.astype(BF16)
                        if di + 1 < nd:
                            plane_ref = stage2_ref.at[u, r]
                            plane_ref[...] = piece
                            nxt.append((r, plane_ref))
                planes = [p for _, p in sorted(nxt, key=lambda t: t[0])]


def _proj_ab(h, g, w_in, wg, layer, wgu, bgu, cos, sin, g_out, seq, tm=512):
    t = h.shape[0]
    row = lambda n: pl.BlockSpec((tm, n), lambda i: (i, 0))
    specs = [row(A_WIDTH)] + [pl.BlockSpec((d, tm // d, A_WIDTH), lambda i: (0, i, 0)) for d in DILATIONS[1:]]
    shapes = [jax.ShapeDtypeStruct((t, A_WIDTH), BF16)]
    shapes += [jax.ShapeDtypeStruct((d, t // d, A_WIDTH), BF16) for d in DILATIONS[1:]]
    specs, shapes = specs * 3, shapes * 3
    n_pairs = N_HEADS_B // 2
    outs = pl.pallas_call(
        functools.partial(_proj_ab_kernel, tiles_per_seq=seq // tm),
        grid=(t // tm,),
        in_specs=[row(D_MODEL), _const_spec((1, D_MODEL)), _layer_spec(w_in, layer), _layer_spec(wg, layer),
                  _layer_spec(wgu, layer), _const_spec(bgu.shape), row(LANES), row(LANES), _const_spec((1, DV_B))],
        out_specs=specs + [row(B_V)],
        out_shape=shapes + [jax.ShapeDtypeStruct((t, B_V), BF16)],
        scratch_shapes=[pltpu.VMEM((tm, D_MODEL), BF16), pltpu.VMEM((3 * N_PAIRS_A, tm, LANES), F32),
                        pltpu.VMEM((3 * N_PAIRS_A, DILATIONS[1], tm // DILATIONS[1], LANES), F32),
                        pltpu.VMEM((tm, B_QK), F32), pltpu.VMEM((tm, B_QK), F32), pltpu.VMEM((tm, B_V), BF16),
                        pltpu.VMEM((tm, B_V), F32), pltpu.VMEM((tm, B_QK), F32),
                        pltpu.VMEM((n_pairs, 2 * DV_B, LANES), F32)] + _gla_scratch(tm) * n_pairs,
        compiler_params=_params(("arbitrary",)),
        name="proj_ab",
    )(h, g, w_in, wg, wgu, bgu, cos, sin, g_out)
    return outs[:3 * len(DILATIONS)], outs[3 * len(DILATIONS)]


def _attn_bias():
    a = np.arange(ATT_BLK)[None, :]
    c = np.arange(2 * ATT_BLK)[:, None]
    band = (c >= a) & (c <= a + ATT_BLK)
    first = band & (c >= ATT_BLK)
    return np.where(np.stack([band, first]), 0.0, NEG).astype(np.float32)


def _attn_kernel(*refs):
    nb = len(DILATIONS)
    in_refs = refs[:3 * nb]
    bias_ref = refs[3 * nb]
    out_ref = refs[3 * nb + 1]
    scratch = refs[3 * nb + 2:]
    acc_refs, lse_refs = scratch[:nb], scratch[nb:2 * nb]
    s_ref, p_ref, inv_ref, stat_ref = scratch[2 * nb:2 * nb + 4]
    kring_refs, vring_refs = scratch[2 * nb + 4:3 * nb + 4], scratch[3 * nb + 4:]
    n = pl.program_id(1)
    n_res = DILATIONS[-1]
    for sub in range(ATT_RES_PER_STEP):
        _attn_substep(n, pl.program_id(2) * ATT_RES_PER_STEP + sub, sub, in_refs, bias_ref, acc_refs, lse_refs,
                      s_ref, p_ref, inv_ref, stat_ref, kring_refs, vring_refs)

    @pl.when(pl.program_id(2) == n_res // ATT_RES_PER_STEP - 1)
    def _():
        chunk = 256
        for hp in range(A_WIDTH // LANES):
            def body(i, carry):
                rows = pl.ds(pl.multiple_of(i * chunk, chunk), chunk)
                lse = [ref[hp, rows, :] for ref in lse_refs]
                top = functools.reduce(jnp.maximum, lse)
                w = [jnp.exp(x - top) for x in lse]
                num = functools.reduce(jnp.add, [wi * ref[hp, rows, :] for wi, ref in zip(w, acc_refs)])
                out_ref[0, rows, hp * LANES:(hp + 1) * LANES] = (num / functools.reduce(jnp.add, w)).astype(BF16)
                return carry
            lax.fori_loop(0, ATT_SPAN // chunk, body, 0)


def _attn_substep(n, r, sub, in_refs, bias_ref, acc_refs, lse_refs, s_ref, p_ref, inv_ref, stat_ref,
                  kring_refs, vring_refs):
    nb = len(DILATIONS)
    n_res = DILATIONS[-1]

    def tile(ref, d):
        return ref.at[0, 0, pl.ds(sub * ATT_BLK, ATT_BLK)] if d == 1 else ref.at[sub, 0]

    lane = lax.broadcasted_iota(jnp.int32, (ATT_BLK, LANES), 1)
    low_head = lane < HEAD_DIM_A
    zero = jnp.zeros((), BF16)
    scale = jnp.asarray(HEAD_DIM_A ** -0.5, BF16)
    eye = jnp.where(lax.broadcasted_iota(jnp.int32, (ATT_BLK, ATT_BLK), 0)
                    == lax.broadcasted_iota(jnp.int32, (ATT_BLK, ATT_BLK), 1), 1.0, 0.0).astype(BF16)
    slots = [r % d for d in DILATIONS]
    blks = [n * (n_res // d) + r // d for d in DILATIONS]

    @pl.when(n == 0)
    def _():
        for g in range(nb):
            @pl.when(blks[g] == 0)
            def _():
                kring_refs[g][slots[g]] = jnp.zeros((ATT_BLK, A_WIDTH), BF16)
                vring_refs[g][slots[g]] = jnp.zeros((ATT_BLK, A_WIDTH), BF16)

    for g, d in enumerate(DILATIONS):
        q_ref, k_ref = tile(in_refs[3 * g], d), tile(in_refs[3 * g + 1], d)
        bias_t = bias_ref[jnp.where(blks[g] == 0, 1, 0)]
        for hp in range(N_PAIRS_A):
            sl = slice(hp * LANES, (hp + 1) * LANES)
            q2 = q_ref[:, sl] * scale
            kcat = jnp.concatenate([kring_refs[g][slots[g], :, sl], k_ref[:, sl]], axis=0)
            rhs = jnp.concatenate([kcat, bias_t], axis=1)
            for hh in range(2):
                qm = jnp.where(low_head == (hh == 0), q2, zero)
                s_ref[(g * N_PAIRS_A + hp) * 2 + hh] = _dot_nt(jnp.concatenate([qm, eye], axis=1), rhs)

    for pair in range(nb * N_PAIRS_A):
        ms, ls = [], []
        for u in (2 * pair, 2 * pair + 1):
            s = s_ref[u]
            m = jnp.max(s, axis=-1, keepdims=True)
            p = jnp.exp(s - m)
            p_ref[u] = p.astype(BF16)
            ms.append(m)
            ls.append(jnp.sum(p, axis=-1, keepdims=True))
        l = jnp.where(low_head, ls[0], ls[1])
        inv_ref[pair] = 1.0 / l
        stat_ref[pair] = jnp.where(low_head, ms[0], ms[1]) + jnp.log(l)

    for g, d in enumerate(DILATIONS):
        v_ref = tile(in_refs[3 * g + 2], d)
        if d == 1:
            rows = pl.ds(pl.multiple_of(r * ATT_BLK, ATT_BLK), ATT_BLK)
        else:
            rows = pl.ds((r // d) * (ATT_BLK * d) + r % d, ATT_BLK, stride=d)
        for hp in range(N_PAIRS_A):
            sl = slice(hp * LANES, (hp + 1) * LANES)
            vcat = jnp.concatenate([vring_refs[g][slots[g], :, sl], v_ref[:, sl]], axis=0)
            pair = g * N_PAIRS_A + hp
            pv = jnp.where(low_head, _dot(p_ref[2 * pair], vcat), _dot(p_ref[2 * pair + 1], vcat))
            acc_refs[g][hp, rows, :] = pv * inv_ref[pair]
            lse_refs[g][hp, rows, :] = stat_ref[pair]
        kring_refs[g][slots[g]] = tile(in_refs[3 * g + 1], d)[...]
        vring_refs[g][slots[g]] = v_ref[...]


def _attention(qkv, batch, seq):
    n_res = DILATIONS[-1]
    n_span = seq // ATT_SPAN
    nd = len(DILATIONS)
    operands, in_specs = [], []
    res = ATT_RES_PER_STEP
    for di, d in enumerate(DILATIONS):
        per = n_res // d
        assert d == 1 or d % res == 0
        if d == 1:
            shape = (1, 1, res * ATT_BLK, A_WIDTH)
            cur = lambda b, n, rs, per=per: (0, b, (n * per) // res + rs, 0)
        else:
            shape = (res, 1, ATT_BLK, A_WIDTH)
            cur = lambda b, n, rs, d=d, per=per: (rs % (d // res), b, n * per + (rs * res) // d, 0)
        operands += [qkv[which * nd + di].reshape(d, batch, seq // d, A_WIDTH) for which in range(3)]
        in_specs += [pl.BlockSpec(shape, cur)] * 3
    bias = jnp.asarray(_attn_bias(), BF16)
    n_units = len(DILATIONS) * N_HEADS_A
    rings = [pltpu.VMEM((d, ATT_BLK, A_WIDTH), BF16) for d in DILATIONS]
    out = pl.pallas_call(
        _attn_kernel,
        grid=(batch, n_span, n_res // res),
        in_specs=in_specs + [_const_spec(bias.shape)],
        out_specs=pl.BlockSpec((1, ATT_SPAN, A_WIDTH), lambda b, n, rs: (b, n, 0)),
        out_shape=jax.ShapeDtypeStruct((batch, seq, A_WIDTH), BF16),
        scratch_shapes=[pltpu.VMEM((N_PAIRS_A, ATT_SPAN, LANES), F32)] * (2 * len(DILATIONS))
        + [pltpu.VMEM((n_units, ATT_BLK, 2 * ATT_BLK), F32), pltpu.VMEM((n_units, ATT_BLK, 2 * ATT_BLK), BF16),
           pltpu.VMEM((n_units // 2, ATT_BLK, LANES), F32), pltpu.VMEM((n_units // 2, ATT_BLK, LANES), F32)]
        + rings + rings,
        compiler_params=_params(("arbitrary", "arbitrary", "arbitrary")),
        name="dilated_attention",
    )(*operands, bias)
    return out.reshape(batch * seq, A_WIDTH)


def _gla_scratch(tm):
    n_chunks = tm // GLA_CHUNK
    return ([pltpu.VMEM((tm, LANES), BF16)] * 4
            + [pltpu.VMEM((n_chunks, LANES), F32), pltpu.VMEM((n_chunks, 2 * DV_B, LANES), F32),
               pltpu.VMEM((n_chunks, 2 * DV_B, LANES), BF16), pltpu.VMEM((n_chunks, GLA_CHUNK, 2 * DV_B), F32),
               pltpu.VMEM((n_chunks, GLA_CHUNK, 2 * GLA_CHUNK), BF16)])


def _gla_phases(q_ref, k_ref, la_ref, v_ref, r_ref, g_ref, o_ref, state_ref,
                qi_ref, ki_ref, kd_ref, qg_ref, dec_ref, kv_ref, sb_ref, oi_ref, att_ref):
    n_chunks = q_ref.shape[0] // GLA_CHUNK
    c = GLA_CHUNK
    two = 2 * c

    def operands():
        row2 = lax.broadcasted_iota(jnp.int32, (two, two), 0)
        col2 = lax.broadcasted_iota(jnp.int32, (two, two), 1)
        tril2 = jnp.where((col2 <= row2) & (col2 // c == row2 // c), 1.0, 0.0).astype(BF16)
        for gi in range(n_chunks // 2):
            _gla_operands(gi, tril2, q_ref, k_ref, la_ref, qi_ref, ki_ref, kd_ref, qg_ref, dec_ref)

    lane_k = lax.broadcasted_iota(jnp.int32, (c, LANES), 1)
    lane_v = lax.broadcasted_iota(jnp.int32, (c, 2 * DV_B), 1)
    zero = jnp.zeros((), BF16)

    def scores():
        att_row = lax.broadcasted_iota(jnp.int32, (c, 2 * c), 0)
        att_col = lax.broadcasted_iota(jnp.int32, (c, 2 * c), 1)
        causal = (att_col % c) <= att_row
        inc_row = lax.broadcasted_iota(jnp.int32, (2 * DV_B, LANES), 0)
        inc_col = lax.broadcasted_iota(jnp.int32, (2 * DV_B, LANES), 1)
        own_block = (inc_row // DV_B) == (inc_col // DK_B)
        for ci in range(n_chunks):
            rows = slice(ci * c, (ci + 1) * c)
            ki = ki_ref[rows, :]
            k_stack = jnp.concatenate([jnp.where(lane_k < DK_B, ki, zero), jnp.where(lane_k >= DK_B, ki, zero)],
                                      axis=0)
            att = jnp.where(causal, _dot_nt(qi_ref[rows, :], k_stack), 0.0)
            att_ref[ci] = att.astype(BF16)
            v_t = v_ref[rows, :].astype(F32).T.astype(BF16)
            kv_ref[ci] = jnp.where(own_block, _dot(v_t, kd_ref[rows, :]), 0.0)

    def intra():
        for ci in range(n_chunks):
            v = v_ref[ci * c:(ci + 1) * c, :]
            v_blocks = jnp.concatenate([jnp.where(lane_v < DV_B, v, zero), jnp.where(lane_v >= DV_B, v, zero)],
                                       axis=0)
            oi_ref[ci] = _dot(att_ref[ci], v_blocks)

    def scan():
        state = state_ref[...]
        for ci in range(n_chunks):
            sb_ref[ci] = state.astype(BF16)
            state = state * dec_ref[ci:ci + 1, :] + kv_ref[ci]
        state_ref[...] = state

    def outputs():
        gain = g_ref[...]
        for ci in range(n_chunks):
            rows = slice(ci * c, (ci + 1) * c)
            o = oi_ref[ci] + _dot_nt(qg_ref[rows, :], sb_ref[ci])
            rg = r_ref[rows, :]
            gate = rg * jax.nn.sigmoid(rg)
            for h in range(2):
                cols = slice(h * DV_B, (h + 1) * DV_B)
                o_ref[rows, cols] = (_rms(o[:, cols]) * gain * gate[:, cols]).astype(BF16)

    return operands, scores, intra, scan, outputs


def _gla_operands(gi, tril2, q_ref, k_ref, la_ref, qi_ref, ki_ref, kd_ref, qg_ref, dec_ref):
    c = GLA_CHUNK
    two = 2 * c
    rows = slice(gi * two, (gi + 1) * two)
    la = la_ref[rows, :]
    hi = la.astype(BF16)
    rem = la - hi.astype(F32)
    mid = rem.astype(BF16)
    lo = (rem - mid.astype(F32)).astype(BF16)
    b3 = _dot(tril2, jnp.concatenate([hi, mid, lo], axis=1))
    b = b3[:, :LANES] + b3[:, LANES:2 * LANES] + b3[:, 2 * LANES:]
    b_last = jnp.concatenate([jnp.broadcast_to(b[c - 1:c], (c, LANES)),
                              jnp.broadcast_to(b[two - 1:two], (c, LANES))], axis=0)
    b_mid = jnp.concatenate([jnp.broadcast_to(b[c // 2 - 1:c // 2], (c, LANES)),
                             jnp.broadcast_to(b[c + c // 2 - 1:c + c // 2], (c, LANES))], axis=0)
    q = q_ref[rows, :] * (DK_B ** -0.5)
    k = k_ref[rows, :]
    qi_ref[rows, :] = (q * jnp.exp(b - b_mid)).astype(BF16)
    ki_ref[rows, :] = (k * jnp.exp(b_mid - b)).astype(BF16)
    kd_ref[rows, :] = (k * jnp.exp(b_last - b)).astype(BF16)
    qg_ref[rows, :] = (q * jnp.exp(b)).astype(BF16)
    dec_ref[2 * gi:2 * gi + 1, :] = jnp.exp(b[c - 1:c])
    dec_ref[2 * gi + 1:2 * gi + 2, :] = jnp.exp(b[two - 1:two])


def _sgu_kernel(h_ref, g_ref, w_in_ref, g_sgu_ref, ws_ref, bias_ref, o1_ref, o2_ref, hn_ref, v_ref, u_ref, *,
                n_chunks):
    d = D_MODEL
    hn_ref[...] = (_rms(h_ref[...]) * g_ref[...]).astype(BF16)
    v = jax.nn.gelu(_dot(hn_ref[...], w_in_ref[:, d:]))
    for cc in range(d // MXU_N):
        cols = slice(cc * MXU_N, (cc + 1) * MXU_N)
        u_ref[:, cols] = jax.nn.gelu(_dot(hn_ref[...], w_in_ref[:, cols]))
    v_ref[...] = (_rms(v) * g_sgu_ref[...]).astype(BF16)
    row = lax.broadcasted_iota(jnp.int32, (SGU_CHUNK, SGU_CHUNK), 0)
    col = lax.broadcasted_iota(jnp.int32, (SGU_CHUNK, SGU_CHUNK), 1)
    tril = col <= row
    half = d // 2
    per_dot = MXU_N // SGU_CHUNK
    for g in range(N_GROUPS_C):
        ws = jnp.where(tril, ws_ref[g], 0.0).astype(BF16)
        cols = slice(g * SGU_CHUNK, (g + 1) * SGU_CHUNK)
        for cc in range(n_chunks // per_dot):
            chunks = range(cc * per_dot, (cc + 1) * per_dot)
            v_cat = jnp.concatenate([v_ref[ci * SGU_CHUNK:(ci + 1) * SGU_CHUNK, cols] for ci in chunks], axis=1)
            sv_cat = _dot(ws, v_cat)
            for k, ci in enumerate(chunks):
                rows = slice(ci * SGU_CHUNK, (ci + 1) * SGU_CHUNK)
                sv = sv_cat[:, k * SGU_CHUNK:(k + 1) * SGU_CHUNK] + bias_ref[:, cols]
                y = (u_ref[rows, cols] * sv).astype(BF16)
                if g < N_GROUPS_C // 2:
                    o1_ref[rows, cols] = y
                else:
                    o2_ref[rows, g * SGU_CHUNK - half:(g + 1) * SGU_CHUNK - half] = y


def _sgu(h, g, w_in, g_sgu, ws, layer, bias, tm=1024):
    t = h.shape[0]
    row = lambda n: pl.BlockSpec((tm, n), lambda i: (i, 0))
    half = D_MODEL // 2
    return pl.pallas_call(
        functools.partial(_sgu_kernel, n_chunks=tm // SGU_CHUNK),
        grid=(t // tm,),
        in_specs=[row(D_MODEL), _const_spec((1, D_MODEL)), _layer_spec(w_in, layer),
                  _const_spec((1, D_MODEL)), _layer_spec(ws, layer), _const_spec(bias.shape)],
        out_specs=[row(half), row(half)],
        out_shape=[jax.ShapeDtypeStruct((t, half), BF16)] * 2,
        scratch_shapes=[pltpu.VMEM((tm, D_MODEL), BF16), pltpu.VMEM((tm, D_MODEL), BF16),
                        pltpu.VMEM((tm, D_MODEL), F32)],
        compiler_params=_params(("arbitrary",)),
        name="sgu",
    )(h, g, w_in, g_sgu, ws, bias)


def _shift_rows(z, prev, k):
    rolled = pltpu.roll(z, k, axis=0)
    head = jnp.where(lax.broadcasted_iota(jnp.int32, prev.shape, 0) < k,
                     pltpu.roll(prev, k, axis=0), rolled[:SUBLANES])
    return jnp.concatenate([head, rolled[SUBLANES:]], axis=0)


def _ffn_kernel(h_ref, o1_ref, o2_ref, w_o_ref, g_ref, w_up_ref, cw_ref, cb_ref, w_down_ref, gf_ref,
                out_ref, hn_ref, act_ref, carry_ref, *, final_norm):
    @pl.when(pl.program_id(1) == 0)
    def _():
        carry_ref[...] = jnp.zeros_like(carry_ref)

    tm = h_ref.shape[0]
    o = jnp.concatenate([o1_ref[...], o2_ref[...]], axis=1)
    h1 = h_ref[...] + _dot(o, w_o_ref[...])
    out_ref[...] = h1
    hn_ref[...] = (_rms(h1) * g_ref[...]).astype(BF16)

    def conv(c, part):
        off = part * D_FF + c * FF_CHUNK
        idx = part * N_FF_CHUNKS + c
        z = _dot(hn_ref[...], w_up_ref[:, off:off + FF_CHUNK])
        prev = carry_ref[idx]
        carry_ref[idx] = z[tm - SUBLANES:]
        w = cw_ref[:, off:off + FF_CHUNK]
        acc = _shift_rows(z, prev, 2) * w[0:1] + _shift_rows(z, prev, 1) * w[1:2] + z * w[2:3]
        return acc + cb_ref[:, off:off + FF_CHUNK]

    for c in range(N_FF_CHUNKS):
        gate = conv(c, 0)
        up = conv(c, 1)
        act_ref[:, c * FF_CHUNK:(c + 1) * FF_CHUNK] = (gate * jax.nn.sigmoid(gate) * up).astype(BF16)
    h2 = out_ref[...] + _dot(act_ref[...], w_down_ref[...])
    if final_norm:
        h2 = _rms(h2) * gf_ref[...]
    out_ref[...] = h2


def _ffn(h, o1, o2, w_o, mixer_layer, g, w_up, conv_w, conv_b, w_down, layer, g_final, batch, seq, final_norm,
         tm=1024):
    t = batch * seq
    per = seq // tm
    row = lambda n: pl.BlockSpec((tm, n), lambda b, i: (b * per + i, 0))
    return pl.pallas_call(
        functools.partial(_ffn_kernel, final_norm=final_norm),
        grid=(batch, per),
        in_specs=[row(D_MODEL), row(D_MODEL // 2), row(D_MODEL // 2), _layer_spec(w_o, mixer_layer),
                  _const_spec((1, D_MODEL)), _layer_spec(w_up, layer), _layer_spec(conv_w, layer),
                  _const_spec(conv_b.shape), _layer_spec(w_down, layer), _const_spec((1, D_MODEL))],
        out_specs=row(D_MODEL),
        out_shape=jax.ShapeDtypeStruct((t, D_MODEL), F32),
        scratch_shapes=[pltpu.VMEM((tm, D_MODEL), BF16), pltpu.VMEM((tm, D_FF), BF16),
                        pltpu.VMEM((2 * N_FF_CHUNKS, SUBLANES, FF_CHUNK), F32)],
        compiler_params=_params(("arbitrary", "arbitrary")),
        name="ffn",
    )(h, o1, o2, w_o, g, w_up, conv_w, conv_b, w_down, g_final)


def kernel(x, positions, norm_mix, norm_ffn, w_in_ab, w_gate_up, b_gate_up, g_out_b, w_out_ab, w_in_c,
           g_sgu, w_spatial, b_spatial, w_out_c, w_up, conv_w, conv_b, w_down, norm_final):
    batch, seq, d = x.shape
    depth = norm_mix.shape[0]
    t = batch * seq
    h = x.reshape(t, d)
    cos, sin = _rope_tables(positions)
    row = lambda v: v.reshape(1, -1)
    w_in_ab_b, w_gate_b = _split_w_in(w_in_ab)
    w_gate_up_b = jnp.pad(w_gate_up, ((0, 0), (0, LANES - GATE_RANK), (0, 0))).astype(BF16)
    w_in_c_b = w_in_c.astype(BF16)
    w_out_b = (w_out_ab.astype(BF16), w_out_c.astype(BF16))
    w_up_b, w_down_b = w_up.astype(BF16), w_down.astype(BF16)
    for layer in range(depth):
        i = layer // 2
        if layer % 2 == 0:
            qkv, o2 = _proj_ab(h, row(norm_mix[layer]), w_in_ab_b, w_gate_b, i, w_gate_up_b, row(b_gate_up[i]), cos, sin,
                               row(g_out_b[i]), seq)
            o1 = _attention(qkv, batch, seq)
        else:
            bias = jnp.repeat(b_spatial[i].T, SGU_CHUNK, axis=1)
            o1, o2 = _sgu(h, row(norm_mix[layer]), w_in_c_b, row(g_sgu[i]), w_spatial, i, bias)
        h = _ffn(h, o1, o2, w_out_b[layer % 2], i, row(norm_ffn[layer]), w_up_b, conv_w, row(conv_b[layer]),
                 w_down_b, layer, row(norm_final), batch, seq, final_norm=(layer == depth - 1))
    return h.reshape(batch, seq, d)
```

```python
import functools

import numpy as np
import jax
import jax.numpy as jnp
from jax import lax
from jax.experimental import pallas as pl
from jax.experimental.pallas import tpu as pltpu

F32 = jnp.float32
BF16 = jnp.bfloat16

D_MODEL = 1024
N_HEADS_A = 8
HEAD_DIM_A = 64
ROT_DIM = HEAD_DIM_A // 4
ROPE_THETA = 500000.0
Q_SCALE_A = float(np.log2(np.e)) * HEAD_DIM_A ** -0.5
DILATIONS = (1, 4, 16)
ATT_BLK = 128
ATT_SPAN = ATT_BLK * DILATIONS[-1]
ATT_RES_PER_STEP = 4
N_HEADS_B = 4
DV_B = 128
DK_B = 64
GATE_RANK = 16
GATE_TAU = 16.0
GLA_CHUNK = 64
N_GROUPS_C = 8
SGU_CHUNK = 128
D_FF = 2816
CONV_W = 3
A_WIDTH = N_HEADS_A * HEAD_DIM_A
B_QK = N_HEADS_B * DK_B
B_V = N_HEADS_B * DV_B
NEG = -1e30
EPS = 1e-6
LANES = 128
SUBLANES = 8
N_PAIRS_A = A_WIDTH // LANES
MXU_N = 256
FF_CHUNK = MXU_N
N_FF_CHUNKS = D_FF // FF_CHUNK

VMEM_LIMIT = 56 * 1024 * 1024


def _dot(a, b):
    return jnp.dot(a, b, preferred_element_type=F32)


def _dot_nt(a, b):
    return lax.dot_general(a, b, (((1,), (1,)), ((), ())), preferred_element_type=F32)


def _rms(x):
    return x * lax.rsqrt(jnp.mean(x * x, axis=-1, keepdims=True) + EPS)


def _const_spec(shape):
    nd = len(shape)
    return pl.BlockSpec(shape, lambda *_: (0,) * nd, pipeline_mode=pl.Buffered(1))


def _layer_spec(stacked, layer):
    nd = stacked.ndim - 1
    return pl.BlockSpec((None,) + stacked.shape[1:], lambda *_: (layer,) + (0,) * nd, pipeline_mode=pl.Buffered(1))


def _params(sem, vmem=VMEM_LIMIT):
    return pltpu.CompilerParams(dimension_semantics=sem, vmem_limit_bytes=vmem)


def _rope_tab_kernel(pos_ref, inv_ref, sgn_ref, c_ref, s_ref):
    ang = pos_ref[...].astype(F32) * inv_ref[...]
    c_ref[...] = jnp.cos(ang)
    s_ref[...] = sgn_ref[...] * jnp.sin(ang)


def _rope_tables(positions):
    t = positions.size
    half = ROT_DIM // 2
    inv = np.float64(ROPE_THETA) ** (-np.arange(half, dtype=np.float64) * (2.0 / ROT_DIM))
    dim = np.arange(LANES) % HEAD_DIM_A
    inv_lane = np.where(dim < ROT_DIM, inv[dim % half], 0.0).astype(np.float32)[None, :]
    sgn_lane = np.where(dim < half, -1.0, np.where(dim < ROT_DIM, 1.0, 0.0)).astype(np.float32)[None, :]
    tm = 1024
    pos = positions.reshape(t, 1)
    return pl.pallas_call(
        _rope_tab_kernel,
        grid=(t // tm,),
        in_specs=[pl.BlockSpec((tm, 1), lambda i: (i, 0)),
                  _const_spec((1, LANES)), _const_spec((1, LANES))],
        out_specs=[pl.BlockSpec((tm, LANES), lambda i: (i, 0))] * 2,
        out_shape=[jax.ShapeDtypeStruct((t, LANES), F32)] * 2,
        compiler_params=_params(("arbitrary",)),
        name="rope_tables",
    )(pos, jnp.asarray(inv_lane), jnp.asarray(sgn_lane))


def _split_w_in_kernel(w_ref, main_ref, gate_ref):
    w = w_ref[0]
    n_main = main_ref.shape[-1]
    main_ref[0] = w[:, :n_main].astype(BF16)
    gate = w[:, n_main:]
    pad = jnp.zeros((gate.shape[0], LANES - gate.shape[1]), F32)
    gate_ref[0] = jnp.concatenate([gate, pad], axis=1).astype(BF16)


def _split_w_in(w_in_ab, tr=256):
    n_layers, d, width = w_in_ab.shape
    n_main = width - GATE_RANK
    return pl.pallas_call(
        _split_w_in_kernel,
        grid=(n_layers, d // tr),
        in_specs=[pl.BlockSpec((1, tr, width), lambda l, i: (l, i, 0))],
        out_specs=[pl.BlockSpec((1, tr, n_main), lambda l, i: (l, i, 0)),
                   pl.BlockSpec((1, tr, LANES), lambda l, i: (l, i, 0))],
        out_shape=[jax.ShapeDtypeStruct((n_layers, d, n_main), BF16),
                   jax.ShapeDtypeStruct((n_layers, d, LANES), BF16)],
        compiler_params=_params(("arbitrary", "arbitrary")),
        name="split_w_in",
    )(w_in_ab)


def _proj_ab_kernel(h_ref, g_ref, w_ref, wg_ref, wgu_ref, bgu_ref, c_ref, s_ref, gb_ref, *refs, tiles_per_seq):
    nd = len(DILATIONS)
    qkv_refs = refs[:3 * nd]
    ob_ref, hn_ref, stage_ref, stage2_ref, qb_ref, kb_ref, vb_ref, rb_ref, la_ref, state_ref = refs[3 * nd:3 * nd + 10]
    gla_scratch = refs[3 * nd + 10:]
    assert nd <= 3
    tm = h_ref.shape[0]
    hn_ref[...] = (_rms(h_ref[...]) * g_ref[...]).astype(BF16)
    cos = c_ref[...]
    sin = s_ref[...]
    lane = lax.broadcasted_iota(jnp.int32, cos.shape, 1)
    first_half = (lane % HEAD_DIM_A) < (ROT_DIM // 2)

    def rope(x):
        partner = jnp.where(first_half,
                            pltpu.roll(x, LANES - ROT_DIM // 2, axis=1),
                            pltpu.roll(x, ROT_DIM // 2, axis=1))
        return x * cos + partner * sin

    def project_a(which):
        for jj in range(A_WIDTH // MXU_N):
            col = which * A_WIDTH + jj * MXU_N
            x2 = _dot(hn_ref[...], w_ref[:, col:col + MXU_N])
            for half in range(MXU_N // LANES):
                j = jj * (MXU_N // LANES) + half
                x = x2[:, half * LANES:(half + 1) * LANES]
                if which < 2:
                    x = rope(x)
                if which == 0:
                    x = x * Q_SCALE_A
                qkv_refs[which * nd][:, j * LANES:(j + 1) * LANES] = x.astype(BF16)
                stage_ref[which * N_PAIRS_A + j] = x

    @pl.when(pl.program_id(0) % tiles_per_seq == 0)
    def _():
        state_ref[...] = jnp.zeros_like(state_ref)

    b0 = 3 * A_WIDTH
    qb_ref[...] = _dot(hn_ref[...], w_ref[:, b0:b0 + B_QK])
    kb_ref[...] = _dot(hn_ref[...], w_ref[:, b0 + B_QK:b0 + 2 * B_QK])
    vb_ref[...] = _dot(hn_ref[...], w_ref[:, b0 + 2 * B_QK:b0 + 2 * B_QK + B_V]).astype(BF16)
    rb_ref[...] = _dot(hn_ref[...], w_ref[:, b0 + 2 * B_QK + B_V:b0 + 2 * B_QK + 2 * B_V])
    gl = _dot(hn_ref[...], wg_ref[...])
    g = _dot(gl.astype(BF16), wgu_ref[...]) + bgu_ref[...]
    la_ref[...] = (jnp.minimum(g, 0.0) - jnp.log1p(jnp.exp(-jnp.abs(g)))) * (1.0 / GATE_TAU)
    n_scr = len(gla_scratch) // (N_HEADS_B // 2)
    pairs = []
    for p in range(N_HEADS_B // 2):
        qk = pl.ds(p * LANES, LANES)
        vr = pl.ds(p * 2 * DV_B, 2 * DV_B)
        pairs.append(_gla_phases(qb_ref.at[:, qk], kb_ref.at[:, qk], la_ref.at[:, qk], vb_ref.at[:, vr],
                                 rb_ref.at[:, vr], gb_ref, ob_ref.at[:, vr], state_ref.at[p],
                                 *gla_scratch[p * n_scr:(p + 1) * n_scr]))
    project_a(0)
    for operands, _, _, _, _ in pairs:
        operands()
    project_a(1)
    for _, scores, _, _, _ in pairs:
        scores()
    project_a(2)
    for _, _, intra, scan, _ in pairs:
        intra()
        scan()
    for _, _, _, _, outputs in pairs:
        outputs()
    for which in range(3):
        for j in range(N_PAIRS_A):
            u = which * N_PAIRS_A + j
            planes = [stage_ref.at[u]]
            for di in range(1, nd):
                ratio = DILATIONS[di] // DILATIONS[di - 1]
                n_rows = tm // DILATIONS[di]
                nxt = []
                for a in range(ratio):
                    for b, plane in enumerate(planes):
                        r = a * len(planes) + b
                        piece = plane[pl.ds(a, n_rows, stride=ratio), :]
                        qkv_refs[which * nd + di][r, :, j * LANES:(j + 1) * LANES] = piece.astype(BF16)
                        if di + 1 < nd:
                            plane_ref = stage2_ref.at[u, r]
                            plane_ref[...] = piece
                            nxt.append((r, plane_ref))
                planes = [p for _, p in sorted(nxt, key=lambda t: t[0])]


def _proj_ab(h, g, w_in, wg, layer, wgu, bgu, cos, sin, g_out, seq, tm=512):
    t = h.shape[0]
    row = lambda n: pl.BlockSpec((tm, n), lambda i: (i, 0))
    specs = [row(A_WIDTH)] + [pl.BlockSpec((d, tm // d, A_WIDTH), lambda i: (0, i, 0)) for d in DILATIONS[1:]]
    shapes = [jax.ShapeDtypeStruct((t, A_WIDTH), BF16)]
    shapes += [jax.ShapeDtypeStruct((d, t // d, A_WIDTH), BF16) for d in DILATIONS[1:]]
    specs, shapes = specs * 3, shapes * 3
    n_pairs = N_HEADS_B // 2
    outs = pl.pallas_call(
        functools.partial(_proj_ab_kernel, tiles_per_seq=seq // tm),
        grid=(t // tm,),
        in_specs=[row(D_MODEL), _const_spec((1, D_MODEL)), _layer_spec(w_in, layer), _layer_spec(wg, layer),
                  _layer_spec(wgu, layer), _const_spec(bgu.shape), row(LANES), row(LANES), _const_spec((1, DV_B))],
        out_specs=specs + [row(B_V)],
        out_shape=shapes + [jax.ShapeDtypeStruct((t, B_V), BF16)],
        scratch_shapes=[pltpu.VMEM((tm, D_MODEL), BF16), pltpu.VMEM((3 * N_PAIRS_A, tm, LANES), F32),
                        pltpu.VMEM((3 * N_PAIRS_A, DILATIONS[1], tm // DILATIONS[1], LANES), F32),
                        pltpu.VMEM((tm, B_QK), F32), pltpu.VMEM((tm, B_QK), F32), pltpu.VMEM((tm, B_V), BF16),
                        pltpu.VMEM((tm, B_V), F32), pltpu.VMEM((tm, B_QK), F32),
                        pltpu.VMEM((n_pairs, 2 * DV_B, LANES), F32)] + _gla_scratch(tm) * n_pairs,
        compiler_params=_params(("arbitrary",)),
        name="proj_ab",
    )(h, g, w_in, wg, wgu, bgu, cos, sin, g_out)
    return outs[:3 * len(DILATIONS)], outs[3 * len(DILATIONS)]


def _attn_bias():
    a = np.arange(ATT_BLK)[None, :]
    c = np.arange(2 * ATT_BLK)[:, None]
    band = (c >= a) & (c <= a + ATT_BLK)
    first = band & (c >= ATT_BLK)
    return np.where(np.stack([band, first]), 0.0, NEG).astype(np.float32)


def _attn_kernel(*refs):
    nb = len(DILATIONS)
    in_refs = refs[:3 * nb]
    bias_ref = refs[3 * nb]
    out_ref = refs[3 * nb + 1]
    scratch = refs[3 * nb + 2:]
    acc_refs, lse_refs = scratch[:nb], scratch[nb:2 * nb]
    s_ref, p_ref, inv_ref, stat_ref = scratch[2 * nb:2 * nb + 4]
    kring_refs, vring_refs = scratch[2 * nb + 4:3 * nb + 4], scratch[3 * nb + 4:]
    n = pl.program_id(1)
    n_res = DILATIONS[-1]
    for sub in range(ATT_RES_PER_STEP):
        _attn_substep(n, pl.program_id(2) * ATT_RES_PER_STEP + sub, sub, in_refs, bias_ref, acc_refs, lse_refs,
                      s_ref, p_ref, inv_ref, stat_ref, kring_refs, vring_refs)

    @pl.when(pl.program_id(2) == n_res // ATT_RES_PER_STEP - 1)
    def _():
        chunk = 256
        for hp in range(A_WIDTH // LANES):
            def body(i, carry):
                rows = pl.ds(pl.multiple_of(i * chunk, chunk), chunk)
                lse = [ref[hp, rows, :] for ref in lse_refs]
                top = functools.reduce(jnp.maximum, lse)
                w = [jnp.exp2(x - top) for x in lse]
                num = functools.reduce(jnp.add, [wi * ref[hp, rows, :] for wi, ref in zip(w, acc_refs)])
                out_ref[0, rows, hp * LANES:(hp + 1) * LANES] = (num / functools.reduce(jnp.add, w)).astype(BF16)
                return carry
            lax.fori_loop(0, ATT_SPAN // chunk, body, 0)


def _attn_substep(n, r, sub, in_refs, bias_ref, acc_refs, lse_refs, s_ref, p_ref, inv_ref, stat_ref,
                  kring_refs, vring_refs):
    nb = len(DILATIONS)
    n_res = DILATIONS[-1]

    def tile(ref, d):
        return ref.at[0, 0, pl.ds(sub * ATT_BLK, ATT_BLK)] if d == 1 else ref.at[sub, 0]

    lane = lax.broadcasted_iota(jnp.int32, (ATT_BLK, LANES), 1)
    low_head = lane < HEAD_DIM_A
    zero = jnp.zeros((), BF16)
    eye = jnp.where(lax.broadcasted_iota(jnp.int32, (ATT_BLK, ATT_BLK), 0)
                    == lax.broadcasted_iota(jnp.int32, (ATT_BLK, ATT_BLK), 1), 1.0, 0.0).astype(BF16)
    slots = [r % d for d in DILATIONS]
    blks = [n * (n_res // d) + r // d for d in DILATIONS]

    @pl.when(n == 0)
    def _():
        for g in range(nb):
            @pl.when(blks[g] == 0)
            def _():
                kring_refs[g][slots[g]] = jnp.zeros((ATT_BLK, A_WIDTH), BF16)
                vring_refs[g][slots[g]] = jnp.zeros((ATT_BLK, A_WIDTH), BF16)

    for g, d in enumerate(DILATIONS):
        q_ref, k_ref = tile(in_refs[3 * g], d), tile(in_refs[3 * g + 1], d)
        bias_t = bias_ref[jnp.where(blks[g] == 0, 1, 0)]
        for hp in range(N_PAIRS_A):
            sl = slice(hp * LANES, (hp + 1) * LANES)
            q2 = q_ref[:, sl]
            kcat = jnp.concatenate([kring_refs[g][slots[g], :, sl], k_ref[:, sl]], axis=0)
            rhs = jnp.concatenate([kcat, bias_t], axis=1)
            for hh in range(2):
                qm = jnp.where(low_head == (hh == 0), q2, zero)
                s_ref[(g * N_PAIRS_A + hp) * 2 + hh] = _dot_nt(jnp.concatenate([qm, eye], axis=1), rhs)

    for pair in range(nb * N_PAIRS_A):
        ms, ls = [], []
        for u in (2 * pair, 2 * pair + 1):
            s = s_ref[u]
            m = jnp.max(s, axis=-1, keepdims=True)
            p = jnp.exp2(s - m)
            p_ref[u] = p.astype(BF16)
            ms.append(m)
            ls.append(jnp.sum(p, axis=-1, keepdims=True))
        l = jnp.where(low_head, ls[0], ls[1])
        inv_ref[pair] = 1.0 / l
        stat_ref[pair] = jnp.where(low_head, ms[0], ms[1]) + jnp.log2(l)

    for g, d in enumerate(DILATIONS):
        v_ref = tile(in_refs[3 * g + 2], d)
        if d == 1:
            rows = pl.ds(pl.multiple_of(r * ATT_BLK, ATT_BLK), ATT_BLK)
        else:
            rows = pl.ds((r // d) * (ATT_BLK * d) + r % d, ATT_BLK, stride=d)
        for hp in range(N_PAIRS_A):
            sl = slice(hp * LANES, (hp + 1) * LANES)
            vcat = jnp.concatenate([vring_refs[g][slots[g], :, sl], v_ref[:, sl]], axis=0)
            pair = g * N_PAIRS_A + hp
            pv = jnp.where(low_head, _dot(p_ref[2 * pair], vcat), _dot(p_ref[2 * pair + 1], vcat))
            acc_refs[g][hp, rows, :] = pv * inv_ref[pair]
            lse_refs[g][hp, rows, :] = stat_ref[pair]
        kring_refs[g][slots[g]] = tile(in_refs[3 * g + 1], d)[...]
        vring_refs[g][slots[g]] = v_ref[...]


def _attention(qkv, batch, seq):
    n_res = DILATIONS[-1]
    n_span = seq // ATT_SPAN
    nd = len(DILATIONS)
    operands, in_specs = [], []
    res = ATT_RES_PER_STEP
    for di, d in enumerate(DILATIONS):
        per = n_res // d
        assert d == 1 or d % res == 0
        if d == 1:
            shape = (1, 1, res * ATT_BLK, A_WIDTH)
            cur = lambda b, n, rs, per=per: (0, b, (n * per) // res + rs, 0)
        else:
            shape = (res, 1, ATT_BLK, A_WIDTH)
            cur = lambda b, n, rs, d=d, per=per: (rs % (d // res), b, n * per + (rs * res) // d, 0)
        operands += [qkv[which * nd + di].reshape(d, batch, seq // d, A_WIDTH) for which in range(3)]
        in_specs += [pl.BlockSpec(shape, cur)] * 3
    bias = jnp.asarray(_attn_bias(), BF16)
    n_units = len(DILATIONS) * N_HEADS_A
    rings = [pltpu.VMEM((d, ATT_BLK, A_WIDTH), BF16) for d in DILATIONS]
    out = pl.pallas_call(
        _attn_kernel,
        grid=(batch, n_span, n_res // res),
        in_specs=in_specs + [_const_spec(bias.shape)],
        out_specs=pl.BlockSpec((1, ATT_SPAN, A_WIDTH), lambda b, n, rs: (b, n, 0)),
        out_shape=jax.ShapeDtypeStruct((batch, seq, A_WIDTH), BF16),
        scratch_shapes=[pltpu.VMEM((N_PAIRS_A, ATT_SPAN, LANES), F32)] * (2 * len(DILATIONS))
        + [pltpu.VMEM((n_units, ATT_BLK, 2 * ATT_BLK), F32), pltpu.VMEM((n_units, ATT_BLK, 2 * ATT_BLK), BF16),
           pltpu.VMEM((n_units // 2, ATT_BLK, LANES), F32), pltpu.VMEM((n_units // 2, ATT_BLK, LANES), F32)]
        + rings + rings,
        compiler_params=_params(("arbitrary", "arbitrary", "arbitrary")),
        name="dilated_attention",
    )(*operands, bias)
    return out.reshape(batch * seq, A_WIDTH)


def _gla_scratch(tm):
    n_chunks = tm // GLA_CHUNK
    return ([pltpu.VMEM((tm, LANES), BF16)] * 4
            + [pltpu.VMEM((n_chunks, LANES), F32), pltpu.VMEM((n_chunks, 2 * DV_B, LANES), F32),
               pltpu.VMEM((n_chunks, 2 * DV_B, LANES), BF16), pltpu.VMEM((n_chunks, GLA_CHUNK, 2 * DV_B), F32),
               pltpu.VMEM((n_chunks, GLA_CHUNK, 2 * GLA_CHUNK), BF16)])


def _gla_phases(q_ref, k_ref, la_ref, v_ref, r_ref, g_ref, o_ref, state_ref,
                qi_ref, ki_ref, kd_ref, qg_ref, dec_ref, kv_ref, sb_ref, oi_ref, att_ref):
    n_chunks = q_ref.shape[0] // GLA_CHUNK
    c = GLA_CHUNK
    two = 2 * c

    def operands():
        row2 = lax.broadcasted_iota(jnp.int32, (two, two), 0)
        col2 = lax.broadcasted_iota(jnp.int32, (two, two), 1)
        tril2 = jnp.where((col2 <= row2) & (col2 // c == row2 // c), 1.0, 0.0).astype(BF16)
        for gi in range(n_chunks // 2):
            _gla_operands(gi, tril2, q_ref, k_ref, la_ref, qi_ref, ki_ref, kd_ref, qg_ref, dec_ref)

    lane_k = lax.broadcasted_iota(jnp.int32, (c, LANES), 1)
    lane_v = lax.broadcasted_iota(jnp.int32, (c, 2 * DV_B), 1)
    zero = jnp.zeros((), BF16)

    def scores():
        att_row = lax.broadcasted_iota(jnp.int32, (c, 2 * c), 0)
        att_col = lax.broadcasted_iota(jnp.int32, (c, 2 * c), 1)
        causal = (att_col % c) <= att_row
        inc_row = lax.broadcasted_iota(jnp.int32, (2 * DV_B, LANES), 0)
        inc_col = lax.broadcasted_iota(jnp.int32, (2 * DV_B, LANES), 1)
        own_block = (inc_row // DV_B) == (inc_col // DK_B)
        for ci in range(n_chunks):
            rows = slice(ci * c, (ci + 1) * c)
            ki = ki_ref[rows, :]
            k_stack = jnp.concatenate([jnp.where(lane_k < DK_B, ki, zero), jnp.where(lane_k >= DK_B, ki, zero)],
                                      axis=0)
            att = jnp.where(causal, _dot_nt(qi_ref[rows, :], k_stack), 0.0)
            att_ref[ci] = att.astype(BF16)
            v_t = v_ref[rows, :].astype(F32).T.astype(BF16)
            kv_ref[ci] = jnp.where(own_block, _dot(v_t, kd_ref[rows, :]), 0.0)

    def intra():
        for ci in range(n_chunks):
            v = v_ref[ci * c:(ci + 1) * c, :]
            v_blocks = jnp.concatenate([jnp.where(lane_v < DV_B, v, zero), jnp.where(lane_v >= DV_B, v, zero)],
                                       axis=0)
            oi_ref[ci] = _dot(att_ref[ci], v_blocks)

    def scan():
        state = state_ref[...]
        for ci in range(n_chunks):
            sb_ref[ci] = state.astype(BF16)
            state = state * dec_ref[ci:ci + 1, :] + kv_ref[ci]
        state_ref[...] = state

    def outputs():
        gain = g_ref[...]
        for ci in range(n_chunks):
            rows = slice(ci * c, (ci + 1) * c)
            o = oi_ref[ci] + _dot_nt(qg_ref[rows, :], sb_ref[ci])
            rg = r_ref[rows, :]
            gate = rg * jax.nn.sigmoid(rg)
            for h in range(2):
                cols = slice(h * DV_B, (h + 1) * DV_B)
                o_ref[rows, cols] = (_rms(o[:, cols]) * gain * gate[:, cols]).astype(BF16)

    return operands, scores, intra, scan, outputs


def _gla_operands(gi, tril2, q_ref, k_ref, la_ref, qi_ref, ki_ref, kd_ref, qg_ref, dec_ref):
    c = GLA_CHUNK
    two = 2 * c
    rows = slice(gi * two, (gi + 1) * two)
    la = la_ref[rows, :]
    hi = la.astype(BF16)
    rem = la - hi.astype(F32)
    mid = rem.astype(BF16)
    lo = (rem - mid.astype(F32)).astype(BF16)
    b3 = _dot(tril2, jnp.concatenate([hi, mid, lo], axis=1))
    b = b3[:, :LANES] + b3[:, LANES:2 * LANES] + b3[:, 2 * LANES:]
    b_last = jnp.concatenate([jnp.broadcast_to(b[c - 1:c], (c, LANES)),
                              jnp.broadcast_to(b[two - 1:two], (c, LANES))], axis=0)
    b_mid = jnp.concatenate([jnp.broadcast_to(b[c // 2 - 1:c // 2], (c, LANES)),
                             jnp.broadcast_to(b[c + c // 2 - 1:c + c // 2], (c, LANES))], axis=0)
    q = q_ref[rows, :] * (DK_B ** -0.5)
    k = k_ref[rows, :]
    qi_ref[rows, :] = (q * jnp.exp(b - b_mid)).astype(BF16)
    ki_ref[rows, :] = (k * jnp.exp(b_mid - b)).astype(BF16)
    kd_ref[rows, :] = (k * jnp.exp(b_last - b)).astype(BF16)
    qg_ref[rows, :] = (q * jnp.exp(b)).astype(BF16)
    dec_ref[2 * gi:2 * gi + 1, :] = jnp.exp(b[c - 1:c])
    dec_ref[2 * gi + 1:2 * gi + 2, :] = jnp.exp(b[two - 1:two])


def _sgu_kernel(h_ref, g_ref, w_in_ref, g_sgu_ref, ws_ref, bias_ref, o1_ref, o2_ref, hn_ref, v_ref, u_ref, *,
                n_chunks):
    d = D_MODEL
    hn_ref[...] = (_rms(h_ref[...]) * g_ref[...]).astype(BF16)
    v = jax.nn.gelu(_dot(hn_ref[...], w_in_ref[:, d:]))
    for cc in range(d // MXU_N):
        cols = slice(cc * MXU_N, (cc + 1) * MXU_N)
        u_ref[:, cols] = jax.nn.gelu(_dot(hn_ref[...], w_in_ref[:, cols]))
    v_ref[...] = (_rms(v) * g_sgu_ref[...]).astype(BF16)
    row = lax.broadcasted_iota(jnp.int32, (SGU_CHUNK, SGU_CHUNK), 0)
    col = lax.broadcasted_iota(jnp.int32, (SGU_CHUNK, SGU_CHUNK), 1)
    tril = col <= row
    half = d // 2
    per_dot = MXU_N // SGU_CHUNK
    for g in range(N_GROUPS_C):
        ws = jnp.where(tril, ws_ref[g], 0.0).astype(BF16)
        cols = slice(g * SGU_CHUNK, (g + 1) * SGU_CHUNK)
        for cc in range(n_chunks // per_dot):
            chunks = range(cc * per_dot, (cc + 1) * per_dot)
            v_cat = jnp.concatenate([v_ref[ci * SGU_CHUNK:(ci + 1) * SGU_CHUNK, cols] for ci in chunks], axis=1)
            sv_cat = _dot(ws, v_cat)
            for k, ci in enumerate(chunks):
                rows = slice(ci * SGU_CHUNK, (ci + 1) * SGU_CHUNK)
                sv = sv_cat[:, k * SGU_CHUNK:(k + 1) * SGU_CHUNK] + bias_ref[:, cols]
                y = (u_ref[rows, cols] * sv).astype(BF16)
                if g < N_GROUPS_C // 2:
                    o1_ref[rows, cols] = y
                else:
                    o2_ref[rows, g * SGU_CHUNK - half:(g + 1) * SGU_CHUNK - half] = y


def _sgu(h, g, w_in, g_sgu, ws, layer, bias, tm=1024):
    t = h.shape[0]
    row = lambda n: pl.BlockSpec((tm, n), lambda i: (i, 0))
    half = D_MODEL // 2
    return pl.pallas_call(
        functools.partial(_sgu_kernel, n_chunks=tm // SGU_CHUNK),
        grid=(t // tm,),
        in_specs=[row(D_MODEL), _const_spec((1, D_MODEL)), _layer_spec(w_in, layer),
                  _const_spec((1, D_MODEL)), _layer_spec(ws, layer), _const_spec(bias.shape)],
        out_specs=[row(half), row(half)],
        out_shape=[jax.ShapeDtypeStruct((t, half), BF16)] * 2,
        scratch_shapes=[pltpu.VMEM((tm, D_MODEL), BF16), pltpu.VMEM((tm, D_MODEL), BF16),
                        pltpu.VMEM((tm, D_MODEL), F32)],
        compiler_params=_params(("arbitrary",)),
        name="sgu",
    )(h, g, w_in, g_sgu, ws, bias)


def _shift_rows(z, prev, k):
    rolled = pltpu.roll(z, k, axis=0)
    head = jnp.where(lax.broadcasted_iota(jnp.int32, prev.shape, 0) < k,
                     pltpu.roll(prev, k, axis=0), rolled[:SUBLANES])
    return jnp.concatenate([head, rolled[SUBLANES:]], axis=0)


def _ffn_kernel(h_ref, o1_ref, o2_ref, w_o_ref, g_ref, w_up_ref, cw_ref, cb_ref, w_down_ref, gf_ref,
                out_ref, hn_ref, act_ref, carry_ref, *, final_norm):
    @pl.when(pl.program_id(1) == 0)
    def _():
        carry_ref[...] = jnp.zeros_like(carry_ref)

    tm = h_ref.shape[0]
    o = jnp.concatenate([o1_ref[...], o2_ref[...]], axis=1)
    h1 = h_ref[...] + _dot(o, w_o_ref[...])
    out_ref[...] = h1
    hn_ref[...] = (_rms(h1) * g_ref[...]).astype(BF16)

    def conv(c, part):
        off = part * D_FF + c * FF_CHUNK
        idx = part * N_FF_CHUNKS + c
        z = _dot(hn_ref[...], w_up_ref[:, off:off + FF_CHUNK])
        prev = carry_ref[idx]
        carry_ref[idx] = z[tm - SUBLANES:]
        w = cw_ref[:, off:off + FF_CHUNK]
        acc = _shift_rows(z, prev, 2) * w[0:1] + _shift_rows(z, prev, 1) * w[1:2] + z * w[2:3]
        return acc + cb_ref[:, off:off + FF_CHUNK]

    for c in range(N_FF_CHUNKS):
        gate = conv(c, 0)
        up = conv(c, 1)
        act_ref[:, c * FF_CHUNK:(c + 1) * FF_CHUNK] = (gate * jax.nn.sigmoid(gate) * up).astype(BF16)
    h2 = out_ref[...] + _dot(act_ref[...], w_down_ref[...])
    if final_norm:
        h2 = _rms(h2) * gf_ref[...]
    out_ref[...] = h2


def _ffn(h, o1, o2, w_o, mixer_layer, g, w_up, conv_w, conv_b, w_down, layer, g_final, batch, seq, final_norm,
         tm=1024):
    t = batch * seq
    per = seq // tm
    row = lambda n: pl.BlockSpec((tm, n), lambda b, i: (b * per + i, 0))
    return pl.pallas_call(
        functools.partial(_ffn_kernel, final_norm=final_norm),
        grid=(batch, per),
        in_specs=[row(D_MODEL), row(D_MODEL // 2), row(D_MODEL // 2), _layer_spec(w_o, mixer_layer),
                  _const_spec((1, D_MODEL)), _layer_spec(w_up, layer), _layer_spec(conv_w, layer),
                  _const_spec(conv_b.shape), _layer_spec(w_down, layer), _const_spec((1, D_MODEL))],
        out_specs=row(D_MODEL),
        out_shape=jax.ShapeDtypeStruct((t, D_MODEL), F32),
        scratch_shapes=[pltpu.VMEM((tm, D_MODEL), BF16), pltpu.VMEM((tm, D_FF), BF16),
                        pltpu.VMEM((2 * N_FF_CHUNKS, SUBLANES, FF_CHUNK), F32)],
        compiler_params=_params(("arbitrary", "arbitrary")),
        name="ffn",
    )(h, o1, o2, w_o, g, w_up, conv_w, conv_b, w_down, g_final)


def kernel(x, positions, norm_mix, norm_ffn, w_in_ab, w_gate_up, b_gate_up, g_out_b, w_out_ab, w_in_c,
           g_sgu, w_spatial, b_spatial, w_out_c, w_up, conv_w, conv_b, w_down, norm_final):
    batch, seq, d = x.shape
    depth = norm_mix.shape[0]
    t = batch * seq
    h = x.reshape(t, d)
    cos, sin = _rope_tables(positions)
    row = lambda v: v.reshape(1, -1)
    w_in_ab_b, w_gate_b = _split_w_in(w_in_ab)
    w_gate_up_b = jnp.pad(w_gate_up, ((0, 0), (0, LANES - GATE_RANK), (0, 0))).astype(BF16)
    w_in_c_b = w_in_c.astype(BF16)
    w_out_b = (w_out_ab.astype(BF16), w_out_c.astype(BF16))
    w_up_b, w_down_b = w_up.astype(BF16), w_down.astype(BF16)
    for layer in range(depth):
        i = layer // 2
        if layer % 2 == 0:
            qkv, o2 = _proj_ab(h, row(norm_mix[layer]), w_in_ab_b, w_gate_b, i, w_gate_up_b, row(b_gate_up[i]), cos, sin,
                               row(g_out_b[i]), seq)
            o1 = _attention(qkv, batch, seq)
        else:
            bias = jnp.repeat(b_spatial[i].T, SGU_CHUNK, axis=1)
            o1, o2 = _sgu(h, row(norm_mix[layer]), w_in_c_b, row(g_sgu[i]), w_spatial, i, bias)
        h = _ffn(h, o1, o2, w_out_b[layer % 2], i, row(norm_ffn[layer]), w_up_b, conv_w, row(conv_b[layer]),
                 w_down_b, layer, row(norm_final), batch, seq, final_norm=(layer == depth - 1))
    return h.reshape(batch, seq, d)
```

```python
import functools

import numpy as np
import jax
import jax.numpy as jnp
from jax import lax
from jax.experimental import pallas as pl
from jax.experimental.pallas import tpu as pltpu

F32 = jnp.float32
BF16 = jnp.bfloat16

D_MODEL = 1024
N_HEADS_A = 8
HEAD_DIM_A = 64
ROT_DIM = HEAD_DIM_A // 4
ROPE_THETA = 500000.0
Q_SCALE_A = float(np.log2(np.e)) * HEAD_DIM_A ** -0.5
DILATIONS = (1, 4, 16)
ATT_BLK = 128
ATT_SPAN = ATT_BLK * DILATIONS[-1]
ATT_SOFTMAX_ROWS = 32
ATT_RES_PER_STEP = 4
N_HEADS_B = 4
DV_B = 128
DK_B = 64
GATE_RANK = 16
GATE_TAU = 16.0
GLA_CHUNK = 64
N_GROUPS_C = 8
SGU_CHUNK = 128
D_FF = 2816
CONV_W = 3
A_WIDTH = N_HEADS_A * HEAD_DIM_A
B_QK = N_HEADS_B * DK_B
B_V = N_HEADS_B * DV_B
NEG = -1e30
EPS = 1e-6
LANES = 128
SUBLANES = 8
N_PAIRS_A = A_WIDTH // LANES
MXU_N = 256
FF_CHUNK = MXU_N
N_FF_CHUNKS = D_FF // FF_CHUNK

VMEM_LIMIT = 56 * 1024 * 1024


def _dot(a, b):
    return jnp.dot(a, b, preferred_element_type=F32)


def _dot_nt(a, b):
    return lax.dot_general(a, b, (((1,), (1,)), ((), ())), preferred_element_type=F32)


def _rms(x):
    return x * lax.rsqrt(jnp.mean(x * x, axis=-1, keepdims=True) + EPS)


def _const_spec(shape):
    nd = len(shape)
    return pl.BlockSpec(shape, lambda *_: (0,) * nd, pipeline_mode=pl.Buffered(1))


def _layer_spec(stacked, layer):
    nd = stacked.ndim - 1
    return pl.BlockSpec((None,) + stacked.shape[1:], lambda *_: (layer,) + (0,) * nd, pipeline_mode=pl.Buffered(1))


def _params(sem, vmem=VMEM_LIMIT):
    return pltpu.CompilerParams(dimension_semantics=sem, vmem_limit_bytes=vmem)


def _rope_tab_kernel(pos_ref, inv_ref, sgn_ref, c_ref, s_ref):
    ang = pos_ref[...].astype(F32) * inv_ref[...]
    c_ref[...] = jnp.cos(ang)
    s_ref[...] = sgn_ref[...] * jnp.sin(ang)


def _rope_tables(positions):
    t = positions.size
    half = ROT_DIM // 2
    inv = np.float64(ROPE_THETA) ** (-np.arange(half, dtype=np.float64) * (2.0 / ROT_DIM))
    dim = np.arange(LANES) % HEAD_DIM_A
    inv_lane = np.where(dim < ROT_DIM, inv[dim % half], 0.0).astype(np.float32)[None, :]
    sgn_lane = np.where(dim < half, -1.0, np.where(dim < ROT_DIM, 1.0, 0.0)).astype(np.float32)[None, :]
    tm = 1024
    pos = positions.reshape(t, 1)
    return pl.pallas_call(
        _rope_tab_kernel,
        grid=(t // tm,),
        in_specs=[pl.BlockSpec((tm, 1), lambda i: (i, 0)),
                  _const_spec((1, LANES)), _const_spec((1, LANES))],
        out_specs=[pl.BlockSpec((tm, LANES), lambda i: (i, 0))] * 2,
        out_shape=[jax.ShapeDtypeStruct((t, LANES), F32)] * 2,
        compiler_params=_params(("arbitrary",)),
        name="rope_tables",
    )(pos, jnp.asarray(inv_lane), jnp.asarray(sgn_lane))


def _split_w_in_kernel(w_ref, main_ref, gate_ref):
    w = w_ref[0]
    n_main = main_ref.shape[-1]
    main_ref[0] = w[:, :n_main].astype(BF16)
    gate = w[:, n_main:]
    pad = jnp.zeros((gate.shape[0], LANES - gate.shape[1]), F32)
    gate_ref[0] = jnp.concatenate([gate, pad], axis=1).astype(BF16)


def _split_w_in(w_in_ab, tr=256):
    n_layers, d, width = w_in_ab.shape
    n_main = width - GATE_RANK
    return pl.pallas_call(
        _split_w_in_kernel,
        grid=(n_layers, d // tr),
        in_specs=[pl.BlockSpec((1, tr, width), lambda l, i: (l, i, 0))],
        out_specs=[pl.BlockSpec((1, tr, n_main), lambda l, i: (l, i, 0)),
                   pl.BlockSpec((1, tr, LANES), lambda l, i: (l, i, 0))],
        out_shape=[jax.ShapeDtypeStruct((n_layers, d, n_main), BF16),
                   jax.ShapeDtypeStruct((n_layers, d, LANES), BF16)],
        compiler_params=_params(("arbitrary", "arbitrary")),
        name="split_w_in",
    )(w_in_ab)


def _proj_ab_kernel(h_ref, g_ref, w_ref, wg_ref, wgu_ref, bgu_ref, c_ref, s_ref, gb_ref, *refs, tiles_per_seq):
    nd = len(DILATIONS)
    qkv_refs = refs[:3 * nd]
    ob_ref, hn_ref, stage_ref, stage2_ref, qb_ref, kb_ref, vb_ref, rb_ref, la_ref, state_ref = refs[3 * nd:3 * nd + 10]
    gla_scratch = refs[3 * nd + 10:]
    assert nd <= 3
    tm = h_ref.shape[0]
    hn_ref[...] = (_rms(h_ref[...]) * g_ref[...]).astype(BF16)
    cos = c_ref[...]
    sin = s_ref[...]
    lane = lax.broadcasted_iota(jnp.int32, cos.shape, 1)
    first_half = (lane % HEAD_DIM_A) < (ROT_DIM // 2)

    def rope(x):
        partner = jnp.where(first_half,
                            pltpu.roll(x, LANES - ROT_DIM // 2, axis=1),
                            pltpu.roll(x, ROT_DIM // 2, axis=1))
        return x * cos + partner * sin

    def project_a(which):
        for jj in range(A_WIDTH // MXU_N):
            col = which * A_WIDTH + jj * MXU_N
            x2 = _dot(hn_ref[...], w_ref[:, col:col + MXU_N])
            for half in range(MXU_N // LANES):
                j = jj * (MXU_N // LANES) + half
                x = x2[:, half * LANES:(half + 1) * LANES]
                if which < 2:
                    x = rope(x)
                if which == 0:
                    x = x * Q_SCALE_A
                qkv_refs[which * nd][:, j * LANES:(j + 1) * LANES] = x.astype(BF16)
                stage_ref[which * N_PAIRS_A + j] = x

    @pl.when(pl.program_id(0) % tiles_per_seq == 0)
    def _():
        state_ref[...] = jnp.zeros_like(state_ref)

    b0 = 3 * A_WIDTH
    qb_ref[...] = _dot(hn_ref[...], w_ref[:, b0:b0 + B_QK])
    kb_ref[...] = _dot(hn_ref[...], w_ref[:, b0 + B_QK:b0 + 2 * B_QK])
    vb_ref[...] = _dot(hn_ref[...], w_ref[:, b0 + 2 * B_QK:b0 + 2 * B_QK + B_V]).astype(BF16)
    rb_ref[...] = _dot(hn_ref[...], w_ref[:, b0 + 2 * B_QK + B_V:b0 + 2 * B_QK + 2 * B_V])
    gl = _dot(hn_ref[...], wg_ref[...])
    g = _dot(gl.astype(BF16), wgu_ref[...]) + bgu_ref[...]
    la_ref[...] = (jnp.minimum(g, 0.0) - jnp.log1p(jnp.exp(-jnp.abs(g)))) * (1.0 / GATE_TAU)
    n_scr = len(gla_scratch) // (N_HEADS_B // 2)
    pairs = []
    for p in range(N_HEADS_B // 2):
        qk = pl.ds(p * LANES, LANES)
        vr = pl.ds(p * 2 * DV_B, 2 * DV_B)
        pairs.append(_gla_phases(qb_ref.at[:, qk], kb_ref.at[:, qk], la_ref.at[:, qk], vb_ref.at[:, vr],
                                 rb_ref.at[:, vr], gb_ref, ob_ref.at[:, vr], state_ref.at[p],
                                 *gla_scratch[p * n_scr:(p + 1) * n_scr]))
    project_a(0)
    for operands, _, _, _, _ in pairs:
        operands()
    project_a(1)
    for _, scores, _, _, _ in pairs:
        scores()
    project_a(2)
    for _, _, intra, scan, _ in pairs:
        intra()
        scan()
    for _, _, _, _, outputs in pairs:
        outputs()
    for which in range(3):
        for j in range(N_PAIRS_A):
            u = which * N_PAIRS_A + j
            planes = [stage_ref.at[u]]
            for di in range(1, nd):
                ratio = DILATIONS[di] // DILATIONS[di - 1]
                n_rows = tm // DILATIONS[di]
                nxt = []
                for a in range(ratio):
                    for b, plane in enumerate(planes):
                        r = a * len(planes) + b
                        piece = plane[pl.ds(a, n_rows, stride=ratio), :]
                        qkv_refs[which * nd + di][r, :, j * LANES:(j + 1) * LANES] = piece.astype(BF16)
                        if di + 1 < nd:
                            plane_ref = stage2_ref.at[u, r]
                            plane_ref[...] = piece
                            nxt.append((r, plane_ref))
                planes = [p for _, p in sorted(nxt, key=lambda t: t[0])]


def _proj_ab(h, g, w_in, wg, layer, wgu, bgu, cos, sin, g_out, seq, tm=512):
    t = h.shape[0]
    row = lambda n: pl.BlockSpec((tm, n), lambda i: (i, 0))
    specs = [row(A_WIDTH)] + [pl.BlockSpec((d, tm // d, A_WIDTH), lambda i: (0, i, 0)) for d in DILATIONS[1:]]
    shapes = [jax.ShapeDtypeStruct((t, A_WIDTH), BF16)]
    shapes += [jax.ShapeDtypeStruct((d, t // d, A_WIDTH), BF16) for d in DILATIONS[1:]]
    specs, shapes = specs * 3, shapes * 3
    n_pairs = N_HEADS_B // 2
    outs = pl.pallas_call(
        functools.partial(_proj_ab_kernel, tiles_per_seq=seq // tm),
        grid=(t // tm,),
        in_specs=[row(D_MODEL), _const_spec((1, D_MODEL)), _layer_spec(w_in, layer), _layer_spec(wg, layer),
                  _layer_spec(wgu, layer), _const_spec(bgu.shape), row(LANES), row(LANES), _const_spec((1, DV_B))],
        out_specs=specs + [row(B_V)],
        out_shape=shapes + [jax.ShapeDtypeStruct((t, B_V), BF16)],
        scratch_shapes=[pltpu.VMEM((tm, D_MODEL), BF16), pltpu.VMEM((3 * N_PAIRS_A, tm, LANES), F32),
                        pltpu.VMEM((3 * N_PAIRS_A, DILATIONS[1], tm // DILATIONS[1], LANES), F32),
                        pltpu.VMEM((tm, B_QK), F32), pltpu.VMEM((tm, B_QK), F32), pltpu.VMEM((tm, B_V), BF16),
                        pltpu.VMEM((tm, B_V), F32), pltpu.VMEM((tm, B_QK), F32),
                        pltpu.VMEM((n_pairs, 2 * DV_B, LANES), F32)] + _gla_scratch(tm) * n_pairs,
        compiler_params=_params(("arbitrary",)),
        name="proj_ab",
    )(h, g, w_in, wg, wgu, bgu, cos, sin, g_out)
    return outs[:3 * len(DILATIONS)], outs[3 * len(DILATIONS)]


def _attn_bias():
    a = np.arange(ATT_BLK)[None, :]
    c = np.arange(2 * ATT_BLK)[:, None]
    band = (c >= a) & (c <= a + ATT_BLK)
    first = band & (c >= ATT_BLK)
    return np.where(np.stack([band, first]), 0.0, NEG).astype(np.float32)


def _attn_kernel(*refs):
    nb = len(DILATIONS)
    in_refs = refs[:3 * nb]
    bias_ref = refs[3 * nb]
    out_ref = refs[3 * nb + 1]
    scratch = refs[3 * nb + 2:]
    acc_refs, lse_refs = scratch[:nb], scratch[nb:2 * nb]
    s_ref, p_ref, inv_ref, stat_ref = scratch[2 * nb:2 * nb + 4]
    kring_refs, vring_refs = scratch[2 * nb + 4:3 * nb + 4], scratch[3 * nb + 4:]
    n = pl.program_id(1)
    n_res = DILATIONS[-1]
    for sub in range(ATT_RES_PER_STEP):
        _attn_substep(n, pl.program_id(2) * ATT_RES_PER_STEP + sub, sub, in_refs, bias_ref, acc_refs, lse_refs,
                      s_ref, p_ref, inv_ref, stat_ref, kring_refs, vring_refs)

    @pl.when(pl.program_id(2) == n_res // ATT_RES_PER_STEP - 1)
    def _():
        chunk = 256
        for hp in range(A_WIDTH // LANES):
            def body(i, carry):
                rows = pl.ds(pl.multiple_of(i * chunk, chunk), chunk)
                lse = [ref[hp, rows, :] for ref in lse_refs]
                top = functools.reduce(jnp.maximum, lse)
                w = [jnp.exp2(x - top) for x in lse]
                num = functools.reduce(jnp.add, [wi * ref[hp, rows, :] for wi, ref in zip(w, acc_refs)])
                out_ref[0, rows, hp * LANES:(hp + 1) * LANES] = (num / functools.reduce(jnp.add, w)).astype(BF16)
                return carry
            lax.fori_loop(0, ATT_SPAN // chunk, body, 0)


def _attn_substep(n, r, sub, in_refs, bias_ref, acc_refs, lse_refs, s_ref, p_ref, inv_ref, stat_ref,
                  kring_refs, vring_refs):
    nb = len(DILATIONS)
    n_res = DILATIONS[-1]

    def tile(ref, d):
        return ref.at[0, 0, pl.ds(sub * ATT_BLK, ATT_BLK)] if d == 1 else ref.at[sub, 0]

    lane = lax.broadcasted_iota(jnp.int32, (ATT_BLK, LANES), 1)
    low_head = lane < HEAD_DIM_A
    zero = jnp.zeros((), BF16)
    eye = jnp.where(lax.broadcasted_iota(jnp.int32, (ATT_BLK, ATT_BLK), 0)
                    == lax.broadcasted_iota(jnp.int32, (ATT_BLK, ATT_BLK), 1), 1.0, 0.0).astype(BF16)
    slots = [r % d for d in DILATIONS]
    blks = [n * (n_res // d) + r // d for d in DILATIONS]

    @pl.when(n == 0)
    def _():
        for g in range(nb):
            @pl.when(blks[g] == 0)
            def _():
                kring_refs[g][slots[g]] = jnp.zeros((ATT_BLK, A_WIDTH), BF16)
                vring_refs[g][slots[g]] = jnp.zeros((ATT_BLK, A_WIDTH), BF16)

    for g, d in enumerate(DILATIONS):
        q_ref, k_ref = tile(in_refs[3 * g], d), tile(in_refs[3 * g + 1], d)
        bias_t = bias_ref[jnp.where(blks[g] == 0, 1, 0)]
        for hp in range(N_PAIRS_A):
            sl = slice(hp * LANES, (hp + 1) * LANES)
            q2 = q_ref[:, sl]
            kcat = jnp.concatenate([kring_refs[g][slots[g], :, sl], k_ref[:, sl]], axis=0)
            rhs = jnp.concatenate([kcat, bias_t], axis=1)
            for hh in range(2):
                qm = jnp.where(low_head == (hh == 0), q2, zero)
                s_ref[(g * N_PAIRS_A + hp) * 2 + hh] = _dot_nt(jnp.concatenate([qm, eye], axis=1), rhs)

    low_rows = lax.broadcasted_iota(jnp.int32, (ATT_SOFTMAX_ROWS, LANES), 1) < HEAD_DIM_A
    for pair in range(nb * N_PAIRS_A):
        for r0 in range(0, ATT_BLK, ATT_SOFTMAX_ROWS):
            rows = slice(r0, r0 + ATT_SOFTMAX_ROWS)
            ms, ls = [], []
            for u in (2 * pair, 2 * pair + 1):
                s = s_ref[u, rows, :]
                m = jnp.max(s, axis=-1, keepdims=True)
                p = jnp.exp2(s - m)
                p_ref[u, rows, :] = p.astype(BF16)
                ms.append(m)
                ls.append(jnp.sum(p, axis=-1, keepdims=True))
            l = jnp.where(low_rows, ls[0], ls[1])
            inv_ref[pair, rows, :] = 1.0 / l
            stat_ref[pair, rows, :] = jnp.where(low_rows, ms[0], ms[1]) + jnp.log2(l)

    for g, d in enumerate(DILATIONS):
        v_ref = tile(in_refs[3 * g + 2], d)
        if d == 1:
            rows = pl.ds(pl.multiple_of(r * ATT_BLK, ATT_BLK), ATT_BLK)
        else:
            rows = pl.ds((r // d) * (ATT_BLK * d) + r % d, ATT_BLK, stride=d)
        for hp in range(N_PAIRS_A):
            sl = slice(hp * LANES, (hp + 1) * LANES)
            vcat = jnp.concatenate([vring_refs[g][slots[g], :, sl], v_ref[:, sl]], axis=0)
            pair = g * N_PAIRS_A + hp
            pv = jnp.where(low_head, _dot(p_ref[2 * pair], vcat), _dot(p_ref[2 * pair + 1], vcat))
            acc_refs[g][hp, rows, :] = pv * inv_ref[pair]
            lse_refs[g][hp, rows, :] = stat_ref[pair]
        kring_refs[g][slots[g]] = tile(in_refs[3 * g + 1], d)[...]
        vring_refs[g][slots[g]] = v_ref[...]


def _attention(qkv, batch, seq):
    n_res = DILATIONS[-1]
    n_span = seq // ATT_SPAN
    nd = len(DILATIONS)
    operands, in_specs = [], []
    res = ATT_RES_PER_STEP
    for di, d in enumerate(DILATIONS):
        per = n_res // d
        assert d == 1 or d % res == 0
        if d == 1:
            shape = (1, 1, res * ATT_BLK, A_WIDTH)
            cur = lambda b, n, rs, per=per: (0, b, (n * per) // res + rs, 0)
        else:
            shape = (res, 1, ATT_BLK, A_WIDTH)
            cur = lambda b, n, rs, d=d, per=per: (rs % (d // res), b, n * per + (rs * res) // d, 0)
        operands += [qkv[which * nd + di].reshape(d, batch, seq // d, A_WIDTH) for which in range(3)]
        in_specs += [pl.BlockSpec(shape, cur)] * 3
    bias = jnp.asarray(_attn_bias(), BF16)
    n_units = len(DILATIONS) * N_HEADS_A
    rings = [pltpu.VMEM((d, ATT_BLK, A_WIDTH), BF16) for d in DILATIONS]
    out = pl.pallas_call(
        _attn_kernel,
        grid=(batch, n_span, n_res // res),
        in_specs=in_specs + [_const_spec(bias.shape)],
        out_specs=pl.BlockSpec((1, ATT_SPAN, A_WIDTH), lambda b, n, rs: (b, n, 0)),
        out_shape=jax.ShapeDtypeStruct((batch, seq, A_WIDTH), BF16),
        scratch_shapes=[pltpu.VMEM((N_PAIRS_A, ATT_SPAN, LANES), F32)] * (2 * len(DILATIONS))
        + [pltpu.VMEM((n_units, ATT_BLK, 2 * ATT_BLK), F32), pltpu.VMEM((n_units, ATT_BLK, 2 * ATT_BLK), BF16),
           pltpu.VMEM((n_units // 2, ATT_BLK, LANES), F32), pltpu.VMEM((n_units // 2, ATT_BLK, LANES), F32)]
        + rings + rings,
        compiler_params=_params(("arbitrary", "arbitrary", "arbitrary")),
        name="dilated_attention",
    )(*operands, bias)
    return out.reshape(batch * seq, A_WIDTH)


def _gla_scratch(tm):
    n_chunks = tm // GLA_CHUNK
    return ([pltpu.VMEM((tm, LANES), BF16)] * 4
            + [pltpu.VMEM((n_chunks, LANES), F32), pltpu.VMEM((n_chunks, 2 * DV_B, LANES), F32),
               pltpu.VMEM((n_chunks, 2 * DV_B, LANES), BF16), pltpu.VMEM((n_chunks, GLA_CHUNK, 2 * DV_B), F32),
               pltpu.VMEM((n_chunks, GLA_CHUNK, 2 * GLA_CHUNK), BF16)])


def _gla_phases(q_ref, k_ref, la_ref, v_ref, r_ref, g_ref, o_ref, state_ref,
                qi_ref, ki_ref, kd_ref, qg_ref, dec_ref, kv_ref, sb_ref, oi_ref, att_ref):
    n_chunks = q_ref.shape[0] // GLA_CHUNK
    c = GLA_CHUNK
    two = 2 * c

    def operands():
        row2 = lax.broadcasted_iota(jnp.int32, (two, two), 0)
        col2 = lax.broadcasted_iota(jnp.int32, (two, two), 1)
        tril2 = jnp.where((col2 <= row2) & (col2 // c == row2 // c), 1.0, 0.0).astype(BF16)
        for gi in range(n_chunks // 2):
            _gla_operands(gi, tril2, q_ref, k_ref, la_ref, qi_ref, ki_ref, kd_ref, qg_ref, dec_ref)

    lane_k = lax.broadcasted_iota(jnp.int32, (c, LANES), 1)
    lane_v = lax.broadcasted_iota(jnp.int32, (c, 2 * DV_B), 1)
    zero = jnp.zeros((), BF16)

    def scores():
        att_row = lax.broadcasted_iota(jnp.int32, (c, 2 * c), 0)
        att_col = lax.broadcasted_iota(jnp.int32, (c, 2 * c), 1)
        causal = (att_col % c) <= att_row
        inc_row = lax.broadcasted_iota(jnp.int32, (2 * DV_B, LANES), 0)
        inc_col = lax.broadcasted_iota(jnp.int32, (2 * DV_B, LANES), 1)
        own_block = (inc_row // DV_B) == (inc_col // DK_B)
        for ci in range(n_chunks):
            rows = slice(ci * c, (ci + 1) * c)
            ki = ki_ref[rows, :]
            k_stack = jnp.concatenate([jnp.where(lane_k < DK_B, ki, zero), jnp.where(lane_k >= DK_B, ki, zero)],
                                      axis=0)
            att = jnp.where(causal, _dot_nt(qi_ref[rows, :], k_stack), 0.0)
            att_ref[ci] = att.astype(BF16)
            v_t = v_ref[rows, :].astype(F32).T.astype(BF16)
            kv_ref[ci] = jnp.where(own_block, _dot(v_t, kd_ref[rows, :]), 0.0)

    def intra():
        for ci in range(n_chunks):
            v = v_ref[ci * c:(ci + 1) * c, :]
            v_blocks = jnp.concatenate([jnp.where(lane_v < DV_B, v, zero), jnp.where(lane_v >= DV_B, v, zero)],
                                       axis=0)
            oi_ref[ci] = _dot(att_ref[ci], v_blocks)

    def scan():
        state = state_ref[...]
        for ci in range(n_chunks):
            sb_ref[ci] = state.astype(BF16)
            state = state * dec_ref[ci:ci + 1, :] + kv_ref[ci]
        state_ref[...] = state

    def outputs():
        gain = g_ref[...]
        for ci in range(n_chunks):
            rows = slice(ci * c, (ci + 1) * c)
            o = oi_ref[ci] + _dot_nt(qg_ref[rows, :], sb_ref[ci])
            rg = r_ref[rows, :]
            gate = rg * jax.nn.sigmoid(rg)
            for h in range(2):
                cols = slice(h * DV_B, (h + 1) * DV_B)
                o_ref[rows, cols] = (_rms(o[:, cols]) * gain * gate[:, cols]).astype(BF16)

    return operands, scores, intra, scan, outputs


def _gla_operands(gi, tril2, q_ref, k_ref, la_ref, qi_ref, ki_ref, kd_ref, qg_ref, dec_ref):
    c = GLA_CHUNK
    two = 2 * c
    rows = slice(gi * two, (gi + 1) * two)
    la = la_ref[rows, :]
    hi = la.astype(BF16)
    rem = la - hi.astype(F32)
    mid = rem.astype(BF16)
    lo = (rem - mid.astype(F32)).astype(BF16)
    b3 = _dot(tril2, jnp.concatenate([hi, mid, lo], axis=1))
    b = b3[:, :LANES] + b3[:, LANES:2 * LANES] + b3[:, 2 * LANES:]
    b_last = jnp.concatenate([jnp.broadcast_to(b[c - 1:c], (c, LANES)),
                              jnp.broadcast_to(b[two - 1:two], (c, LANES))], axis=0)
    b_mid = jnp.concatenate([jnp.broadcast_to(b[c // 2 - 1:c // 2], (c, LANES)),
                             jnp.broadcast_to(b[c + c // 2 - 1:c + c // 2], (c, LANES))], axis=0)
    q = q_ref[rows, :] * (DK_B ** -0.5)
    k = k_ref[rows, :]
    qi_ref[rows, :] = (q * jnp.exp(b - b_mid)).astype(BF16)
    ki_ref[rows, :] = (k * jnp.exp(b_mid - b)).astype(BF16)
    kd_ref[rows, :] = (k * jnp.exp(b_last - b)).astype(BF16)
    qg_ref[rows, :] = (q * jnp.exp(b)).astype(BF16)
    dec_ref[2 * gi:2 * gi + 1, :] = jnp.exp(b[c - 1:c])
    dec_ref[2 * gi + 1:2 * gi + 2, :] = jnp.exp(b[two - 1:two])


def _sgu_kernel(h_ref, g_ref, w_in_ref, g_sgu_ref, ws_ref, bias_ref, o1_ref, o2_ref, hn_ref, v_ref, u_ref, *,
                n_chunks):
    d = D_MODEL
    hn_ref[...] = (_rms(h_ref[...]) * g_ref[...]).astype(BF16)
    v = jax.nn.gelu(_dot(hn_ref[...], w_in_ref[:, d:]))
    for cc in range(d // MXU_N):
        cols = slice(cc * MXU_N, (cc + 1) * MXU_N)
        u_ref[:, cols] = jax.nn.gelu(_dot(hn_ref[...], w_in_ref[:, cols]))
    v_ref[...] = (_rms(v) * g_sgu_ref[...]).astype(BF16)
    row = lax.broadcasted_iota(jnp.int32, (SGU_CHUNK, SGU_CHUNK), 0)
    col = lax.broadcasted_iota(jnp.int32, (SGU_CHUNK, SGU_CHUNK), 1)
    tril = col <= row
    half = d // 2
    per_dot = MXU_N // SGU_CHUNK
    for g in range(N_GROUPS_C):
        ws = jnp.where(tril, ws_ref[g], 0.0).astype(BF16)
        cols = slice(g * SGU_CHUNK, (g + 1) * SGU_CHUNK)
        for cc in range(n_chunks // per_dot):
            chunks = range(cc * per_dot, (cc + 1) * per_dot)
            v_cat = jnp.concatenate([v_ref[ci * SGU_CHUNK:(ci + 1) * SGU_CHUNK, cols] for ci in chunks], axis=1)
            sv_cat = _dot(ws, v_cat)
            for k, ci in enumerate(chunks):
                rows = slice(ci * SGU_CHUNK, (ci + 1) * SGU_CHUNK)
                sv = sv_cat[:, k * SGU_CHUNK:(k + 1) * SGU_CHUNK] + bias_ref[:, cols]
                y = (u_ref[rows, cols] * sv).astype(BF16)
                if g < N_GROUPS_C // 2:
                    o1_ref[rows, cols] = y
                else:
                    o2_ref[rows, g * SGU_CHUNK - half:(g + 1) * SGU_CHUNK - half] = y


def _sgu(h, g, w_in, g_sgu, ws, layer, bias, tm=1024):
    t = h.shape[0]
    row = lambda n: pl.BlockSpec((tm, n), lambda i: (i, 0))
    half = D_MODEL // 2
    return pl.pallas_call(
        functools.partial(_sgu_kernel, n_chunks=tm // SGU_CHUNK),
        grid=(t // tm,),
        in_specs=[row(D_MODEL), _const_spec((1, D_MODEL)), _layer_spec(w_in, layer),
                  _const_spec((1, D_MODEL)), _layer_spec(ws, layer), _const_spec(bias.shape)],
        out_specs=[row(half), row(half)],
        out_shape=[jax.ShapeDtypeStruct((t, half), BF16)] * 2,
        scratch_shapes=[pltpu.VMEM((tm, D_MODEL), BF16), pltpu.VMEM((tm, D_MODEL), BF16),
                        pltpu.VMEM((tm, D_MODEL), F32)],
        compiler_params=_params(("arbitrary",)),
        name="sgu",
    )(h, g, w_in, g_sgu, ws, bias)


def _shift_rows(z, prev, k):
    rolled = pltpu.roll(z, k, axis=0)
    head = jnp.where(lax.broadcasted_iota(jnp.int32, prev.shape, 0) < k,
                     pltpu.roll(prev, k, axis=0), rolled[:SUBLANES])
    return jnp.concatenate([head, rolled[SUBLANES:]], axis=0)


def _ffn_kernel(h_ref, o1_ref, o2_ref, w_o_ref, g_ref, w_up_ref, cw_ref, cb_ref, w_down_ref, gf_ref,
                out_ref, hn_ref, act_ref, carry_ref, *, final_norm):
    @pl.when(pl.program_id(1) == 0)
    def _():
        carry_ref[...] = jnp.zeros_like(carry_ref)

    tm = h_ref.shape[0]
    o = jnp.concatenate([o1_ref[...], o2_ref[...]], axis=1)
    h1 = h_ref[...] + _dot(o, w_o_ref[...])
    out_ref[...] = h1
    hn_ref[...] = (_rms(h1) * g_ref[...]).astype(BF16)

    def conv(c, part):
        off = part * D_FF + c * FF_CHUNK
        idx = part * N_FF_CHUNKS + c
        z = _dot(hn_ref[...], w_up_ref[:, off:off + FF_CHUNK])
        prev = carry_ref[idx]
        carry_ref[idx] = z[tm - SUBLANES:]
        w = cw_ref[:, off:off + FF_CHUNK]
        acc = _shift_rows(z, prev, 2) * w[0:1] + _shift_rows(z, prev, 1) * w[1:2] + z * w[2:3]
        return acc + cb_ref[:, off:off + FF_CHUNK]

    for c in range(N_FF_CHUNKS):
        gate = conv(c, 0)
        up = conv(c, 1)
        act_ref[:, c * FF_CHUNK:(c + 1) * FF_CHUNK] = (gate * jax.nn.sigmoid(gate) * up).astype(BF16)
    h2 = out_ref[...] + _dot(act_ref[...], w_down_ref[...])
    if final_norm:
        h2 = _rms(h2) * gf_ref[...]
    out_ref[...] = h2


def _ffn(h, o1, o2, w_o, mixer_layer, g, w_up, conv_w, conv_b, w_down, layer, g_final, batch, seq, final_norm,
         tm=1024):
    t = batch * seq
    per = seq // tm
    row = lambda n: pl.BlockSpec((tm, n), lambda b, i: (b * per + i, 0))
    return pl.pallas_call(
        functools.partial(_ffn_kernel, final_norm=final_norm),
        grid=(batch, per),
        in_specs=[row(D_MODEL), row(D_MODEL // 2), row(D_MODEL // 2), _layer_spec(w_o, mixer_layer),
                  _const_spec((1, D_MODEL)), _layer_spec(w_up, layer), _layer_spec(conv_w, layer),
                  _const_spec(conv_b.shape), _layer_spec(w_down, layer), _const_spec((1, D_MODEL))],
        out_specs=row(D_MODEL),
        out_shape=jax.ShapeDtypeStruct((t, D_MODEL), F32),
        scratch_shapes=[pltpu.VMEM((tm, D_MODEL), BF16), pltpu.VMEM((tm, D_FF), BF16),
                        pltpu.VMEM((2 * N_FF_CHUNKS, SUBLANES, FF_CHUNK), F32)],
        compiler_params=_params(("arbitrary", "arbitrary")),
        name="ffn",
    )(h, o1, o2, w_o, g, w_up, conv_w, conv_b, w_down, g_final)


def kernel(x, positions, norm_mix, norm_ffn, w_in_ab, w_gate_up, b_gate_up, g_out_b, w_out_ab, w_in_c,
           g_sgu, w_spatial, b_spatial, w_out_c, w_up, conv_w, conv_b, w_down, norm_final):
    batch, seq, d = x.shape
    depth = norm_mix.shape[0]
    t = batch * seq
    h = x.reshape(t, d)
    cos, sin = _rope_tables(positions)
    row = lambda v: v.reshape(1, -1)
    w_in_ab_b, w_gate_b = _split_w_in(w_in_ab)
    w_gate_up_b = jnp.pad(w_gate_up, ((0, 0), (0, LANES - GATE_RANK), (0, 0))).astype(BF16)
    w_in_c_b = w_in_c.astype(BF16)
    w_out_b = (w_out_ab.astype(BF16), w_out_c.astype(BF16))
    w_up_b, w_down_b = w_up.astype(BF16), w_down.astype(BF16)
    for layer in range(depth):
        i = layer // 2
        if layer % 2 == 0:
            qkv, o2 = _proj_ab(h, row(norm_mix[layer]), w_in_ab_b, w_gate_b, i, w_gate_up_b, row(b_gate_up[i]), cos, sin,
                               row(g_out_b[i]), seq)
            o1 = _attention(qkv, batch, seq)
        else:
            bias = jnp.repeat(b_spatial[i].T, SGU_CHUNK, axis=1)
            o1, o2 = _sgu(h, row(norm_mix[layer]), w_in_c_b, row(g_sgu[i]), w_spatial, i, bias)
        h = _ffn(h, o1, o2, w_out_b[layer % 2], i, row(norm_ffn[layer]), w_up_b, conv_w, row(conv_b[layer]),
                 w_down_b, layer, row(norm_final), batch, seq, final_norm=(layer == depth - 1))
    return h.reshape(batch, seq, d)
```
